```python
import jax, jax.numpy as jnp
from jax import lax
import numpy as np

D_MODEL = 1024
BATCH = 8
SEQ = 2048
DEPTH = 4

N_MIXERS = 2
MIX_HEADS = 6
HEAD_DIM = 128
MIX_WIDTH = MIX_HEADS * HEAD_DIM
N_XATTN_HEADS = 4
XATTN_HEAD_DIM = 64
XATTN_WIDTH = N_XATTN_HEADS * XATTN_HEAD_DIM
N_MEM = 256
IN_WIDTH = 4 * MIX_WIDTH + XATTN_WIDTH
CAT_WIDTH = MIX_WIDTH + XATTN_WIDTH
D_FF = -(-(8 * D_MODEL) // (3 * 256)) * 256
N_HGRN_LAYERS = (DEPTH + N_MIXERS - 1) // N_MIXERS
N_RET_LAYERS = DEPTH - N_HGRN_LAYERS
HGRN_CHUNK = 32
RET_CHUNK = 64
ROPE_BASE = 10000.0
EPS = 1e-6
EXP_CLAMP = 30.0

kernel_name = "hgrn2_retention_interleaved_memxattn"


def rmsnorm(x, w):
    xf = x.astype(jnp.float32)
    y = xf * lax.rsqrt(jnp.mean(xf * xf, axis=-1, keepdims=True) + EPS)
    return (y * w.astype(jnp.float32)).astype(x.dtype)


def to_heads(t, h):
    b, s, _ = t.shape
    return t.reshape(b, s, h, -1)


def to_chunks(t, c):
    b, s, h, d = t.shape
    return t.reshape(b, s // c, c, h, d).transpose(0, 3, 1, 2, 4)


def from_chunks(t):
    b, h, n, c, d = t.shape
    return t.transpose(0, 2, 3, 1, 4).reshape(b, n * c, h, d)


def rope(t, positions):
    half = t.shape[-1] // 2
    inv_freq = ROPE_BASE ** (-jnp.linspace(0.0, 1.0, half, dtype=jnp.float32))
    ang = positions.astype(jnp.float32)[:, :, None] * inv_freq
    cos, sin = jnp.cos(ang)[:, :, None, :], jnp.sin(ang)[:, :, None, :]
    t1, t2 = t[..., :half].astype(jnp.float32), t[..., half:].astype(jnp.float32)
    return jnp.concatenate([t1 * cos - t2 * sin, t1 * sin + t2 * cos], axis=-1)


def hgrn2_chunked(q, k, v, g):
    c = HGRN_CHUNK
    q, k, v, g = (to_chunks(t, c) for t in (q, k, v, g))
    q, k = q.astype(jnp.float32), k.astype(jnp.float32)
    b = jnp.cumsum(g.astype(jnp.float32), axis=3)
    ref = b[:, :, :, c // 2 - 1:c // 2, :]
    b_last = b[:, :, :, -1:, :]
    scores = jnp.einsum('bhnck,bhnsk->bhncs', q * jnp.exp(b - ref), k * jnp.exp(ref - b))
    causal = jnp.tril(jnp.ones((c, c), dtype=bool))
    scores = jnp.where(causal, scores, 0.0)
    o_intra = jnp.einsum('bhncs,bhnsv->bhncv', scores, v)
    q_out = jnp.moveaxis(q * jnp.exp(b), 2, 0)
    k_st = jnp.moveaxis(k * jnp.exp(b_last - b), 2, 0)
    v_c = jnp.moveaxis(v, 2, 0)
    dec = jnp.moveaxis(jnp.exp(b_last[:, :, :, 0, :]), 2, 0)

    def step(s, xs):
        qo, ks, vc, dc = xs
        o = jnp.einsum('bhck,bhkv->bhcv', qo, s)
        s = s * dc[..., None] + jnp.einsum('bhck,bhcv->bhkv', ks, vc)
        return s, o

    bsz, h, _, _, dk = q.shape
    s0 = jnp.zeros((bsz, h, dk, v.shape[-1]), jnp.float32)
    _, o_inter = lax.scan(step, s0, (q_out, k_st, v_c, dec))
    return from_chunks(o_intra + jnp.moveaxis(o_inter, 0, 2))


def retention_chunked(q, k, v, log_gamma):
    c = RET_CHUNK
    q, k, v = (to_chunks(t, c) for t in (q, k, v))
    pos = jnp.arange(c, dtype=jnp.float32)
    lg = log_gamma[:, None]
    rel = pos[:, None] - pos[None, :]
    decay_mask = jnp.where(rel >= 0, jnp.exp(lg[:, :, None] * jnp.maximum(rel, 0.0)), 0.0)
    scores = jnp.einsum('bhncd,bhnsd->bhncs', q, k) * decay_mask[None, :, None]
    o_intra = jnp.einsum('bhncs,bhnsv->bhncv', scores, v)
    q_dec = jnp.exp(lg * (pos + 1.0))[None, :, :, None]
    k_dec = jnp.exp(lg * (c - 1.0 - pos))[None, :, :, None]
    chunk_dec = jnp.exp(log_gamma * c)[None, :, None, None]

    def step(s, xs):
        qc, kc, vc = xs
        o = jnp.einsum('bhcd,bhdv->bhcv', qc, s) * q_dec
        s = s * chunk_dec + jnp.einsum('bhcd,bhcv->bhdv', kc * k_dec, vc)
        return s, o

    bsz, h, _, _, dk = q.shape
    s0 = jnp.zeros((bsz, h, dk, v.shape[-1]), jnp.float32)
    xs = (jnp.moveaxis(q, 2, 0), jnp.moveaxis(k, 2, 0), jnp.moveaxis(v, 2, 0))
    _, o_inter = lax.scan(step, s0, xs)
    return from_chunks(o_intra + jnp.moveaxis(o_inter, 0, 2))


def memory_cross_attention(xq, mem_k, mem_v):
    s = jnp.einsum('bthd,bmhd->bhtm', xq, mem_k).astype(jnp.float32) * (XATTN_HEAD_DIM ** -0.5)
    p = jax.nn.softmax(s, axis=-1)
    return jnp.einsum('bhtm,bmhd->bthd', p.astype(mem_v.dtype), mem_v)


def setup_inputs(seed: int = 0) -> dict:
    key = jax.random.key(seed)
    ks = jax.random.split(key, 20)
    f32 = jnp.float32

    def w(k, shape, fan_in):
        return jax.random.normal(k, shape, f32) * (fan_in ** -0.5)

    def gain(k, shape):
        return 1.0 + 0.02 * jax.random.normal(k, shape, f32)

    offset = jax.random.randint(ks[2], (BATCH, 1), 0, 1024, dtype=jnp.int32)
    positions = offset + jnp.arange(SEQ, dtype=jnp.int32)[None, :]
    return {
        "x": jax.random.normal(ks[0], (BATCH, SEQ, D_MODEL), f32),
        "mem": jax.random.normal(ks[1], (BATCH, N_MEM, D_MODEL), f32),
        "positions": positions,
        "norm_mix": gain(ks[3], (DEPTH, D_MODEL)),
        "w_in": w(ks[4], (DEPTH, D_MODEL, IN_WIDTH), D_MODEL),
        "w_out": w(ks[5], (DEPTH, CAT_WIDTH, D_MODEL), CAT_WIDTH),
        "norm_mem": gain(ks[6], (DEPTH, D_MODEL)),
        "w_mem_kv": w(ks[7], (DEPTH, D_MODEL, 2 * XATTN_WIDTH), D_MODEL),
        "hgrn_lb_logits": 0.1 * jax.random.normal(ks[8], (N_HGRN_LAYERS, MIX_WIDTH), f32),
        "hgrn_out_norm": gain(ks[9], (N_HGRN_LAYERS, MIX_WIDTH)),
        "ret_out_norm": gain(ks[10], (N_RET_LAYERS, MIX_HEADS, HEAD_DIM)),
        "norm_ffn": gain(ks[11], (DEPTH, D_MODEL)),
        "w_ffn_in": w(ks[12], (DEPTH, D_MODEL, 2 * D_FF), D_MODEL),
        "w_ffn_out": w(ks[13], (DEPTH, D_FF, D_MODEL), D_FF),
        "norm_final": gain(ks[14], (D_MODEL,)),
    }


def reference(x, mem, positions, norm_mix, w_in, w_out, norm_mem, w_mem_kv, hgrn_lb_logits,
              hgrn_out_norm, ret_out_norm, norm_ffn, w_ffn_in, w_ffn_out, norm_final):
    bsz, seq, _ = x.shape
    p_lb = jax.nn.softmax(hgrn_lb_logits.astype(jnp.float32), axis=0)
    lower_bounds = jnp.cumsum(p_lb, axis=0) - p_lb[0]
    log_gamma = jnp.log(1.0 - 2.0 ** (-5.0 - jnp.arange(MIX_HEADS, dtype=jnp.float32)))

    for i in range(DEPTH):
        h = rmsnorm(x, norm_mix[i])
        z = h @ w_in[i]
        za = z[..., 0 * MIX_WIDTH:1 * MIX_WIDTH]
        zb = z[..., 1 * MIX_WIDTH:2 * MIX_WIDTH]
        zc = z[..., 2 * MIX_WIDTH:3 * MIX_WIDTH]
        zg = z[..., 3 * MIX_WIDTH:4 * MIX_WIDTH]
        zx = z[..., 4 * MIX_WIDTH:]

        if i % N_MIXERS == 0:
            j = i // N_MIXERS
            lb = lower_bounds[j]
            fr = zb.astype(jnp.float32)
            g = jax.nn.log_sigmoid(fr) + jnp.log1p(lb * jnp.exp(jnp.minimum(-fr, EXP_CLAMP)))
            k_in = (1.0 - lb) * jax.nn.sigmoid(-fr)
            o = hgrn2_chunked(to_heads(jax.nn.silu(za), MIX_HEADS), to_heads(k_in, MIX_HEADS),
                              to_heads(zc, MIX_HEADS), to_heads(g, MIX_HEADS))
            o = o.reshape(bsz, seq, MIX_WIDTH)
            mix = rmsnorm(o, hgrn_out_norm[j]) * jax.nn.sigmoid(zg.astype(jnp.float32))
        else:
            j = i // N_MIXERS
            q = rope(to_heads(za, MIX_HEADS), positions)
            k = rope(to_heads(zb, MIX_HEADS), positions) * (HEAD_DIM ** -0.5)
            v = to_heads(zc, MIX_HEADS).astype(jnp.float32)
            o = retention_chunked(q, k, v, log_gamma)
            o = rmsnorm(o, ret_out_norm[j]).reshape(bsz, seq, MIX_WIDTH)
            mix = o * jax.nn.silu(zg.astype(jnp.float32))

        memh = rmsnorm(mem, norm_mem[i])
        mkv = memh @ w_mem_kv[i]
        mem_k = to_heads(mkv[..., :XATTN_WIDTH], N_XATTN_HEADS)
        mem_v = to_heads(mkv[..., XATTN_WIDTH:], N_XATTN_HEADS)
        xo = memory_cross_attention(to_heads(zx, N_XATTN_HEADS), mem_k, mem_v)
        xo = xo.reshape(bsz, seq, XATTN_WIDTH)

        cat = jnp.concatenate([mix.astype(x.dtype), xo.astype(x.dtype)], axis=-1)
        x = x + cat @ w_out[i]

        hf = rmsnorm(x, norm_ffn[i])
        gu = hf @ w_ffn_in[i]
        x = x + (jax.nn.silu(gu[..., :D_FF]) * gu[..., D_FF:]) @ w_ffn_out[i]

    return rmsnorm(x, norm_final)
```

```python
import functools
import math

import jax
import jax.numpy as jnp
from jax import lax
from jax.experimental import pallas as pl
from jax.experimental.pallas import tpu as pltpu

D_MODEL = 1024
MIX_HEADS = 6
HEAD_DIM = 128
MIX_WIDTH = MIX_HEADS * HEAD_DIM
N_XATTN_HEADS = 4
XATTN_HEAD_DIM = 64
XATTN_WIDTH = N_XATTN_HEADS * XATTN_HEAD_DIM
N_MEM = 256
IN_WIDTH = 4 * MIX_WIDTH + XATTN_WIDTH
CAT_WIDTH = MIX_WIDTH + XATTN_WIDTH
D_FF = 2816
N_MIXERS = 2
ROPE_BASE = 10000.0
EPS = 1e-6
EXP_CLAMP = 30.0

SEQ_TILE = 512
HGRN_BLOCK = 128
HGRN_SUB = 32
RET_BLOCK = 256
FFN_TILE = 512
FFN_CHUNK = 1408
VMEM_LIMIT_BYTES = 60 * 1024 * 1024

_BF16 = jnp.bfloat16
_F32 = jnp.float32


def _dot(a, b):
    return jnp.dot(a, b, preferred_element_type=_F32)


def _dot_nt(a, b):
    return lax.dot_general(a, b, (((1,), (1,)), ((), ())), preferred_element_type=_F32)


def _dot_tn(a, b):
    return lax.dot_general(a, b, (((0,), (0,)), ((), ())), preferred_element_type=_F32)


def _group_index(idx, group):
    shift = group.bit_length() - 1
    assert 1 << shift == group
    return lax.shift_right_logical(idx, jnp.int32(shift))


def _rmsnorm(x, w):
    return x * lax.rsqrt(jnp.mean(x * x, axis=-1, keepdims=True) + EPS) * w


def _cumsum_rows(x):
    n = x.shape[0]
    row = lax.broadcasted_iota(jnp.int32, x.shape, 0)
    s = 1
    while s < n:
        x = x + jnp.where(row >= s, pltpu.roll(x, s, 0), 0.0)
        s *= 2
    return x


def _memory_kv(mem_ref, nmem_ref, wkv_ref, mkT_ref, mv_ref):
    memh = _rmsnorm(mem_ref[0], nmem_ref[...]).astype(_BF16)
    mkv = _dot(memh, wkv_ref[...])
    mk = mkv[:, :XATTN_WIDTH] * (XATTN_HEAD_DIM ** -0.5)
    mv = mkv[:, XATTN_WIDTH:]
    lane_head = _group_index(lax.broadcasted_iota(jnp.int32, (N_MEM, XATTN_WIDTH), 1), XATTN_HEAD_DIM)
    for h in range(N_XATTN_HEADS):
        mkT_ref[h] = jnp.where(lane_head == h, mk, 0.0).T.astype(_BF16)
        mv_ref[h] = jnp.where(lane_head == h, mv, 0.0).astype(_BF16)


def _cross_attention_rows(z_ref, rows, mkT_ref, mv_ref, cat_ref):
    qx = z_ref[rows, 4 * MIX_WIDTH:].astype(_BF16)
    acc = None
    for h in range(N_XATTN_HEADS):
        s = _dot(qx, mkT_ref[h])
        e = jnp.exp(s - jnp.max(s, axis=-1, keepdims=True))
        p = (e / jnp.sum(e, axis=-1, keepdims=True)).astype(_BF16)
        o = _dot(p, mv_ref[h])
        acc = o if acc is None else acc + o
    cat_ref[rows, MIX_WIDTH:] = acc.astype(_BF16)


def _hgrn_rows(z_ref, rows, lb_ref, onorm_ref, st_ref, cat_ref):
    n = HGRN_BLOCK
    row = lax.broadcasted_iota(jnp.int32, (n, HEAD_DIM), 0)
    sub = _group_index(row, HGRN_SUB)
    ti = lax.broadcasted_iota(jnp.int32, (n, n), 0)
    si = lax.broadcasted_iota(jnp.int32, (n, n), 1)
    mask_a = (_group_index(ti, HGRN_SUB) == _group_index(si, HGRN_SUB)) & (si <= ti)
    mask_b = _group_index(ti, 2 * HGRN_SUB) == _group_index(si, 2 * HGRN_SUB)

    outs = []
    for h in range(MIX_HEADS):
        hs = slice(h * HEAD_DIM, (h + 1) * HEAD_DIM)
        za = z_ref[rows, hs]
        fr = z_ref[rows, MIX_WIDTH + h * HEAD_DIM:MIX_WIDTH + (h + 1) * HEAD_DIM]
        v = z_ref[rows, 2 * MIX_WIDTH + h * HEAD_DIM:2 * MIX_WIDTH + (h + 1) * HEAD_DIM]
        lb = lb_ref[:, hs]
        q = za * jax.nn.sigmoid(za)
        log_sig = jnp.minimum(fr, 0.0) - jnp.log1p(jnp.exp(-jnp.abs(fr)))
        g = log_sig + jnp.log1p(lb * jnp.exp(jnp.minimum(-fr, EXP_CLAMP)))
        k = (1.0 - lb) * jax.nn.sigmoid(-fr)
        b = _cumsum_rows(g)

        mids = [b[c * HGRN_SUB + HGRN_SUB // 2 - 1:c * HGRN_SUB + HGRN_SUB // 2, :] for c in range(4)]
        ends = [b[(c + 1) * HGRN_SUB - 1:(c + 1) * HGRN_SUB, :] for c in range(4)]
        zero = jnp.zeros_like(mids[0])

        def per_sub(vals):
            out = vals[3]
            for c in (2, 1, 0):
                out = jnp.where(sub == c, vals[c], out)
            return out

        ref = per_sub(mids)
        q_a = q * jnp.exp(b - ref)
        k_a = k * jnp.exp(ref - b)
        q_b = q_a * per_sub([zero, jnp.exp(mids[1] - ends[0]), zero, jnp.exp(mids[3] - ends[2])])
        k_b = k_a * per_sub([jnp.exp(ends[0] - mids[0]), zero, jnp.exp(ends[2] - mids[2]), zero])
        q_c = q_a * per_sub([zero, zero, jnp.exp(mids[2] - ends[1]), jnp.exp(mids[3] - ends[1])])
        k_c = k_a * per_sub([jnp.exp(ends[1] - mids[0]), jnp.exp(ends[1] - mids[1]), zero, zero])
        q_o = q_a * jnp.exp(ref)
        k_s = k_a * per_sub([jnp.exp(ends[3] - m) for m in mids])

        s_a = _dot_nt(q_a.astype(_BF16), k_a.astype(_BF16))
        s_b = _dot_nt(q_b.astype(_BF16), k_b.astype(_BF16))
        s_c = _dot_nt(q_c.astype(_BF16), k_c.astype(_BF16))
        scores = jnp.where(mask_a, s_a, 0.0) + jnp.where(mask_b, s_b, 0.0) + s_c
        vb = v.astype(_BF16)
        st = st_ref[h]
        o = _dot(scores.astype(_BF16), vb) + _dot_nt(q_o.astype(_BF16), st.astype(_BF16))
        st_ref[h] = st * jnp.exp(ends[3]) + _dot_tn(vb, k_s.astype(_BF16))
        outs.append(o)

    ss = None
    for o in outs:
        part = jnp.sum(o * o, axis=-1, keepdims=True)
        ss = part if ss is None else ss + part
    inv = lax.rsqrt(ss * (1.0 / MIX_WIDTH) + EPS)
    for h, o in enumerate(outs):
        hs = slice(h * HEAD_DIM, (h + 1) * HEAD_DIM)
        zg = z_ref[rows, 3 * MIX_WIDTH + h * HEAD_DIM:3 * MIX_WIDTH + (h + 1) * HEAD_DIM]
        cat_ref[rows, hs] = (o * inv * onorm_ref[:, hs] * jax.nn.sigmoid(zg)).astype(_BF16)


def _retention_rows(z_ref, rows, cs_ref, onorm_ref, st_ref, cat_ref):
    n = RET_BLOCK
    cos2 = cs_ref[0, rows, :]
    sin2 = cs_ref[1, rows, :]
    ti = lax.broadcasted_iota(jnp.int32, (n, n), 0)
    si = lax.broadcasted_iota(jnp.int32, (n, n), 1)
    rel = (ti - si).astype(_F32)
    causal = ti >= si
    pos = lax.broadcasted_iota(jnp.int32, (n, HEAD_DIM), 0).astype(_F32)

    def rope(t):
        return t * cos2 + pltpu.roll(t, HEAD_DIM // 2, 1) * sin2

    for h in range(MIX_HEADS):
        hs = slice(h * HEAD_DIM, (h + 1) * HEAD_DIM)
        lg = math.log(1.0 - 2.0 ** (-5.0 - h))
        q = rope(z_ref[rows, hs])
        k = rope(z_ref[rows, MIX_WIDTH + h * HEAD_DIM:MIX_WIDTH + (h + 1) * HEAD_DIM]) * (HEAD_DIM ** -0.5)
        v = z_ref[rows, 2 * MIX_WIDTH + h * HEAD_DIM:2 * MIX_WIDTH + (h + 1) * HEAD_DIM]
        zg = z_ref[rows, 3 * MIX_WIDTH + h * HEAD_DIM:3 * MIX_WIDTH + (h + 1) * HEAD_DIM]
        qb = q.astype(_BF16)
        vb = v.astype(_BF16)
        decay = jnp.where(causal, jnp.exp(lg * jnp.maximum(rel, 0.0)), 0.0)
        scores = _dot_nt(qb, k.astype(_BF16)) * decay
        st = st_ref[h]
        o = _dot(scores.astype(_BF16), vb) + _dot_nt(qb, st.astype(_BF16)) * jnp.exp(lg * (pos + 1.0))
        k_s = k * jnp.exp(lg * (n - 1.0 - pos))
        st_ref[h] = st * math.exp(lg * n) + _dot_tn(vb, k_s.astype(_BF16))
        inv = lax.rsqrt(jnp.mean(o * o, axis=-1, keepdims=True) + EPS)
        cat_ref[rows, hs] = (o * inv * onorm_ref[:, hs] * (zg * jax.nn.sigmoid(zg))).astype(_BF16)


def _mixer_kernel(*refs, kind):
    if kind == "hgrn":
        (x_ref, mem_ref, nmix_ref, win_ref, wout_ref, nmem_ref, wkv_ref, lb_ref, onorm_ref,
         out_ref, z_ref, cat_ref, st_ref, mkT_ref, mv_ref) = refs
    else:
        (x_ref, mem_ref, nmix_ref, win_ref, wout_ref, nmem_ref, wkv_ref, pos_ref, freq_ref, onorm_ref,
         out_ref, z_ref, cat_ref, st_ref, mkT_ref, mv_ref, cs_ref) = refs

    @pl.when(pl.program_id(1) == 0)
    def _start_of_sequence():
        st_ref[...] = jnp.zeros_like(st_ref)
        _memory_kv(mem_ref, nmem_ref, wkv_ref, mkT_ref, mv_ref)

    x = x_ref[0]
    hn = _rmsnorm(x, nmix_ref[...]).astype(_BF16)
    z_ref[...] = _dot(hn, win_ref[...])

    if kind == "hgrn":
        block = HGRN_BLOCK
    else:
        block = RET_BLOCK
        ang = pos_ref[0].astype(_F32) * freq_ref[...]
        lane = lax.broadcasted_iota(jnp.int32, ang.shape, 1)
        cs_ref[0] = jnp.cos(ang)
        cs_ref[1] = jnp.where(lane < HEAD_DIM // 2, -1.0, 1.0) * jnp.sin(ang)

    def body(i, carry):
        rows = pl.ds(pl.multiple_of(i * block, block), block)
        if kind == "hgrn":
            _hgrn_rows(z_ref, rows, lb_ref, onorm_ref, st_ref, cat_ref)
        else:
            _retention_rows(z_ref, rows, cs_ref, onorm_ref, st_ref, cat_ref)
        _cross_attention_rows(z_ref, rows, mkT_ref, mv_ref, cat_ref)
        return carry

    lax.fori_loop(0, SEQ_TILE // block, body, 0)
    out_ref[0] = x + _dot(cat_ref[...], wout_ref[...])


def _mixer_layer(kind, x, mem, norm_mix, w_in, w_out, norm_mem, w_mem_kv, extra):
    bsz, seq, _ = x.shape
    const2 = lambda b, t: (0, 0)
    in_specs = [
        pl.BlockSpec((1, SEQ_TILE, D_MODEL), lambda b, t: (b, t, 0)),
        pl.BlockSpec((1, N_MEM, D_MODEL), lambda b, t: (b, 0, 0)),
        pl.BlockSpec((1, D_MODEL), const2),
        pl.BlockSpec((D_MODEL, IN_WIDTH), const2),
        pl.BlockSpec((CAT_WIDTH, D_MODEL), const2),
        pl.BlockSpec((1, D_MODEL), const2),
        pl.BlockSpec((D_MODEL, 2 * XATTN_WIDTH), const2),
    ]
    scratch = [
        pltpu.VMEM((SEQ_TILE, IN_WIDTH), _F32),
        pltpu.VMEM((SEQ_TILE, CAT_WIDTH), _BF16),
        pltpu.VMEM((MIX_HEADS, HEAD_DIM, HEAD_DIM), _F32),
        pltpu.VMEM((N_XATTN_HEADS, XATTN_WIDTH, N_MEM), _BF16),
        pltpu.VMEM((N_XATTN_HEADS, N_MEM, XATTN_WIDTH), _BF16),
    ]
    if kind == "hgrn":
        in_specs += [pl.BlockSpec((1, MIX_WIDTH), const2), pl.BlockSpec((1, MIX_WIDTH), const2)]
    else:
        in_specs += [
            pl.BlockSpec((1, SEQ_TILE, 1), lambda b, t: (b, t, 0)),
            pl.BlockSpec((1, HEAD_DIM), const2),
            pl.BlockSpec((1, MIX_WIDTH), const2),
        ]
        scratch.append(pltpu.VMEM((2, SEQ_TILE, HEAD_DIM), _F32))
    return pl.pallas_call(
        functools.partial(_mixer_kernel, kind=kind),
        grid=(bsz, seq // SEQ_TILE),
        in_specs=in_specs,
        out_specs=pl.BlockSpec((1, SEQ_TILE, D_MODEL), lambda b, t: (b, t, 0)),
        out_shape=jax.ShapeDtypeStruct(x.shape, x.dtype),
        scratch_shapes=scratch,
        compiler_params=pltpu.CompilerParams(
            dimension_semantics=("arbitrary", "arbitrary"), vmem_limit_bytes=VMEM_LIMIT_BYTES),
        name=f"{kind}_mixer_layer",
    )(x, mem, norm_mix, w_in, w_out, norm_mem, w_mem_kv, *extra)


def _ffn_kernel(*refs, final):
    if final:
        x_ref, n_ref, win_ref, wout_ref, nf_ref, out_ref = refs
    else:
        x_ref, n_ref, win_ref, wout_ref, out_ref = refs
    x = x_ref[...]
    hn = _rmsnorm(x, n_ref[...]).astype(_BF16)
    acc = x
    for c in range(D_FF // FFN_CHUNK):
        g = _dot(hn, win_ref[:, c * FFN_CHUNK:(c + 1) * FFN_CHUNK])
        u = _dot(hn, win_ref[:, D_FF + c * FFN_CHUNK:D_FF + (c + 1) * FFN_CHUNK])
        a = (g * jax.nn.sigmoid(g) * u).astype(_BF16)
        acc = acc + _dot(a, wout_ref[c * FFN_CHUNK:(c + 1) * FFN_CHUNK, :])
    if final:
        acc = _rmsnorm(acc, nf_ref[...])
    out_ref[...] = acc


def _ffn_layer(x2d, norm_ffn, w_ffn_in, w_ffn_out, norm_final):
    final = norm_final is not None
    const = lambda i: (0, 0)
    in_specs = [
        pl.BlockSpec((FFN_TILE, D_MODEL), lambda i: (i, 0)),
        pl.BlockSpec((1, D_MODEL), const),
        pl.BlockSpec((D_MODEL, 2 * D_FF), const),
        pl.BlockSpec((D_FF, D_MODEL), const),
    ]
    args = [x2d, norm_ffn, w_ffn_in, w_ffn_out]
    if final:
        in_specs.append(pl.BlockSpec((1, D_MODEL), const))
        args.append(norm_final)
    return pl.pallas_call(
        functools.partial(_ffn_kernel, final=final),
        grid=(x2d.shape[0] // FFN_TILE,),
        in_specs=in_specs,
        out_specs=pl.BlockSpec((FFN_TILE, D_MODEL), lambda i: (i, 0)),
        out_shape=jax.ShapeDtypeStruct(x2d.shape, x2d.dtype),
        compiler_params=pltpu.CompilerParams(
            dimension_semantics=("arbitrary",), vmem_limit_bytes=VMEM_LIMIT_BYTES),
        name="swiglu_ffn_final" if final else "swiglu_ffn",
    )(*args)


def kernel(x, mem, positions, norm_mix, w_in, w_out, norm_mem, w_mem_kv, hgrn_lb_logits, hgrn_out_norm,
           ret_out_norm, norm_ffn, w_ffn_in, w_ffn_out, norm_final):
    bsz, seq, _ = x.shape
    depth = w_in.shape[0]
    assert seq % SEQ_TILE == 0 and (bsz * seq) % FFN_TILE == 0

    p_lb = jax.nn.softmax(hgrn_lb_logits.astype(_F32), axis=0)
    lower_bounds = jnp.cumsum(p_lb, axis=0) - p_lb[0]
    half = HEAD_DIM // 2
    inv_freq = ROPE_BASE ** (-jnp.linspace(0.0, 1.0, half, dtype=_F32))
    inv_freq2 = jnp.concatenate([inv_freq, inv_freq]).reshape(1, HEAD_DIM)
    pos3 = positions.reshape(bsz, seq, 1)

    for i in range(depth):
        j = i // N_MIXERS
        common = (x, mem, norm_mix[i].reshape(1, D_MODEL), w_in[i].astype(_BF16), w_out[i].astype(_BF16),
                  norm_mem[i].reshape(1, D_MODEL), w_mem_kv[i].astype(_BF16))
        if i % N_MIXERS == 0:
            extra = (lower_bounds[j].reshape(1, MIX_WIDTH), hgrn_out_norm[j].reshape(1, MIX_WIDTH))
            x = _mixer_layer("hgrn", *common, extra)
        else:
            extra = (pos3, inv_freq2, ret_out_norm[j].reshape(1, MIX_WIDTH))
            x = _mixer_layer("retention", *common, extra)
        x = _ffn_layer(
            x.reshape(bsz * seq, D_MODEL), norm_ffn[i].reshape(1, D_MODEL), w_ffn_in[i].astype(_BF16),
            w_ffn_out[i].astype(_BF16), norm_final.reshape(1, D_MODEL) if i == depth - 1 else None,
        ).reshape(bsz, seq, D_MODEL)
    return x
```

```python
import functools
import math

import jax
import jax.numpy as jnp
from jax import lax
from jax.experimental import pallas as pl
from jax.experimental.pallas import tpu as pltpu

D_MODEL = 1024
MIX_HEADS = 6
HEAD_DIM = 128
MIX_WIDTH = MIX_HEADS * HEAD_DIM
N_XATTN_HEADS = 4
XATTN_HEAD_DIM = 64
XATTN_WIDTH = N_XATTN_HEADS * XATTN_HEAD_DIM
N_MEM = 256
IN_WIDTH = 4 * MIX_WIDTH + XATTN_WIDTH
CAT_WIDTH = MIX_WIDTH + XATTN_WIDTH
D_FF = 2816
N_MIXERS = 2
ROPE_BASE = 10000.0
EPS = 1e-6
EXP_CLAMP = 30.0

SEQ_TILE = 512
PROJ_CHUNK = 256
HGRN_BLOCK = 128
HGRN_SUB = 32
N_SUB = HGRN_BLOCK // HGRN_SUB
RET_BLOCK = 256
FFN_TILE = 512
FFN_CHUNK = 1408
VMEM_LIMIT_BYTES = 60 * 1024 * 1024

_BF16 = jnp.bfloat16
_F32 = jnp.float32


def _dot(a, b):
    return jnp.dot(a, b, preferred_element_type=_F32)


def _dot_nt(a, b):
    return lax.dot_general(a, b, (((1,), (1,)), ((), ())), preferred_element_type=_F32)


def _dot_tn(a, b):
    return lax.dot_general(a, b, (((0,), (0,)), ((), ())), preferred_element_type=_F32)


def _group_index(idx, group):
    shift = group.bit_length() - 1
    assert 1 << shift == group
    return lax.shift_right_logical(idx, jnp.int32(shift))


def _rmsnorm(x, w):
    return x * lax.rsqrt(jnp.mean(x * x, axis=-1, keepdims=True) + EPS) * w


def _memory_kv(mem_ref, nmem_ref, wkv_ref, mkT_ref, mv_ref):
    memh = _rmsnorm(mem_ref[0], nmem_ref[...]).astype(_BF16)
    mkv = _dot(memh, wkv_ref[...])
    mk = mkv[:, :XATTN_WIDTH] * (XATTN_HEAD_DIM ** -0.5)
    mv = mkv[:, XATTN_WIDTH:]
    lane_head = _group_index(lax.broadcasted_iota(jnp.int32, (N_MEM, XATTN_WIDTH), 1), XATTN_HEAD_DIM)
    for h in range(N_XATTN_HEADS):
        mkT_ref[h] = jnp.where(lane_head == h, mk, 0.0).T.astype(_BF16)
        mv_ref[h] = jnp.where(lane_head == h, mv, 0.0).astype(_BF16)


def _cross_attention_rows(z_ref, rows, mkT_ref, mv_ref, cat_ref):
    qx = z_ref[rows, 4 * MIX_WIDTH:].astype(_BF16)
    acc = None
    for h in range(N_XATTN_HEADS):
        s = _dot(qx, mkT_ref[h])
        e = jnp.exp(s - jnp.max(s, axis=-1, keepdims=True))
        p = (e / jnp.sum(e, axis=-1, keepdims=True)).astype(_BF16)
        o = _dot(p, mv_ref[h])
        acc = o if acc is None else acc + o
    cat_ref[rows, MIX_WIDTH:] = acc.astype(_BF16)


def _sub_rows(a, c):
    return a[c * HGRN_SUB:(c + 1) * HGRN_SUB]


def _scale_subs(a, vecs):
    parts = []
    for c, vec in enumerate(vecs):
        part = jnp.zeros((HGRN_SUB, a.shape[1]), _F32) if vec is None else _sub_rows(a, c) * vec
        parts.append(part.astype(_BF16))
    return jnp.concatenate(parts, axis=0)


def _hgrn_rows(z_ref, rows, lb_ref, onorm_ref, st_ref, cat_ref):
    n = HGRN_BLOCK
    ti = lax.broadcasted_iota(jnp.int32, (n, n), 0)
    si = lax.broadcasted_iota(jnp.int32, (n, n), 1)
    mask_a = (_group_index(ti, HGRN_SUB) == _group_index(si, HGRN_SUB)) & (si <= ti)
    mask_b = _group_index(ti, 2 * HGRN_SUB) == _group_index(si, 2 * HGRN_SUB)
    lower_ones = jnp.where(si <= ti, 1.0, 0.0).astype(_BF16)
    exp_clamp = math.exp(EXP_CLAMP)

    heads = range(MIX_HEADS)
    head_lanes = [slice(h * HEAD_DIM, (h + 1) * HEAD_DIM) for h in heads]

    za = z_ref[rows, :MIX_WIDTH]
    fr = z_ref[rows, MIX_WIDTH:2 * MIX_WIDTH]
    lb = lb_ref[...]
    q = za / (1.0 + jnp.exp(-za))
    e = jnp.exp(-jnp.abs(fr))
    r = 1.0 / (1.0 + e)
    nonneg = fr >= 0.0
    k = (1.0 - lb) * (jnp.where(nonneg, e, 1.0) * r)
    e_neg = jnp.where(nonneg, e, jnp.minimum(1.0 / e, exp_clamp))
    g = jnp.minimum(fr, 0.0) + jnp.log((1.0 + lb * e_neg) * r)
    g_hi = g.astype(_BF16)
    g_lo = (g - g_hi.astype(_F32)).astype(_BF16)
    b = _dot(lower_ones, g_hi) + _dot(lower_ones, g_lo)

    mids = [b[c * HGRN_SUB + HGRN_SUB // 2 - 1:c * HGRN_SUB + HGRN_SUB // 2, :] for c in range(N_SUB)]
    ends = [b[(c + 1) * HGRN_SUB - 1:(c + 1) * HGRN_SUB, :] for c in range(N_SUB)]
    ref = jnp.concatenate([jnp.broadcast_to(m, (HGRN_SUB, MIX_WIDTH)) for m in mids], axis=0)
    q_a = q * jnp.exp(b - ref)
    k_a = k * jnp.exp(ref - b)
    q_b = _scale_subs(q_a, [None, jnp.exp(mids[1] - ends[0]), None, jnp.exp(mids[3] - ends[2])])
    k_b = _scale_subs(k_a, [jnp.exp(ends[0] - mids[0]), None, jnp.exp(ends[2] - mids[2]), None])
    q_c = _scale_subs(q_a, [None, None, jnp.exp(mids[2] - ends[1]), jnp.exp(mids[3] - ends[1])])
    k_c = _scale_subs(k_a, [jnp.exp(ends[1] - mids[0]), jnp.exp(ends[1] - mids[1]), None, None])
    q_o = _scale_subs(q_a, [jnp.exp(m) for m in mids])
    k_s = _scale_subs(k_a, [jnp.exp(ends[N_SUB - 1] - m) for m in mids])
    q_a = q_a.astype(_BF16)
    k_a = k_a.astype(_BF16)
    state_decay = jnp.exp(ends[N_SUB - 1])
    vb = z_ref[rows, 2 * MIX_WIDTH:3 * MIX_WIDTH].astype(_BF16)

    outs = []
    for h, hs in zip(heads, head_lanes):
        s_a = _dot_nt(q_a[:, hs], k_a[:, hs])
        s_b = _dot_nt(q_b[:, hs], k_b[:, hs])
        s_c = _dot_nt(q_c[:, hs], k_c[:, hs])
        scores = jnp.where(mask_a, s_a, 0.0) + jnp.where(mask_b, s_b, 0.0) + s_c
        st = st_ref[h]
        outs.append(_dot(scores.astype(_BF16), vb[:, hs]) + _dot_nt(q_o[:, hs], st.astype(_BF16)))
        st_ref[h] = st * state_decay[:, hs] + _dot_tn(vb[:, hs], k_s[:, hs])

    ss = None
    for o in outs:
        part = jnp.sum(o * o, axis=-1, keepdims=True)
        ss = part if ss is None else ss + part
    inv = lax.rsqrt(ss * (1.0 / MIX_WIDTH) + EPS)
    for h, o in enumerate(outs):
        hs = slice(h * HEAD_DIM, (h + 1) * HEAD_DIM)
        zg = z_ref[rows, 3 * MIX_WIDTH + h * HEAD_DIM:3 * MIX_WIDTH + (h + 1) * HEAD_DIM]
        cat_ref[rows, hs] = (o * (inv * onorm_ref[:, hs]) / (1.0 + jnp.exp(-zg))).astype(_BF16)


def _retention_tables(dec_ref, qdec_ref, kdec_ref):
    n = RET_BLOCK
    scale = HEAD_DIM ** -0.5
    ti = lax.broadcasted_iota(jnp.int32, (n, n), 0)
    si = lax.broadcasted_iota(jnp.int32, (n, n), 1)
    rel = (ti - si).astype(_F32)
    pos = lax.broadcasted_iota(jnp.int32, (n, HEAD_DIM), 0).astype(_F32)
    for h in range(MIX_HEADS):
        lg = math.log(1.0 - 2.0 ** (-5.0 - h))
        dec_ref[h] = jnp.where(ti >= si, jnp.exp(lg * jnp.maximum(rel, 0.0)) * scale, 0.0)
        qdec_ref[h] = jnp.exp(lg * (pos + 1.0))
        kdec_ref[h] = jnp.exp(lg * (n - 1.0 - pos)) * scale


def _retention_rows(z_ref, rows, cs_ref, onorm_ref, st_ref, cat_ref, dec_ref, qdec_ref, kdec_ref):
    n = RET_BLOCK
    cos2 = cs_ref[0, rows, :]
    sin2 = cs_ref[1, rows, :]

    def rope(t):
        return t * cos2 + pltpu.roll(t, HEAD_DIM // 2, 1) * sin2

    for h in range(MIX_HEADS):
        hs = slice(h * HEAD_DIM, (h + 1) * HEAD_DIM)
        lg = math.log(1.0 - 2.0 ** (-5.0 - h))
        q = rope(z_ref[rows, hs])
        k = rope(z_ref[rows, MIX_WIDTH + h * HEAD_DIM:MIX_WIDTH + (h + 1) * HEAD_DIM])
        v = z_ref[rows, 2 * MIX_WIDTH + h * HEAD_DIM:2 * MIX_WIDTH + (h + 1) * HEAD_DIM]
        zg = z_ref[rows, 3 * MIX_WIDTH + h * HEAD_DIM:3 * MIX_WIDTH + (h + 1) * HEAD_DIM]
        qb = q.astype(_BF16)
        vb = v.astype(_BF16)
        scores = _dot_nt(qb, k.astype(_BF16)) * dec_ref[h]
        st = st_ref[h]
        o = _dot(scores.astype(_BF16), vb) + _dot_nt(qb, st.astype(_BF16)) * qdec_ref[h]
        k_s = k * kdec_ref[h]
        st_ref[h] = st * math.exp(lg * n) + _dot_tn(vb, k_s.astype(_BF16))
        inv = lax.rsqrt(jnp.mean(o * o, axis=-1, keepdims=True) + EPS)
        cat_ref[rows, hs] = (o * (inv * onorm_ref[:, hs]) * (zg / (1.0 + jnp.exp(-zg)))).astype(_BF16)


def _mixer_kernel(*refs, kind):
    if kind == "hgrn":
        (x_ref, mem_ref, nmix_ref, win_ref, wout_ref, nmem_ref, wkv_ref, lb_ref, onorm_ref,
         out_ref, z_ref, cat_ref, st_ref, mkT_ref, mv_ref) = refs
    else:
        (x_ref, mem_ref, nmix_ref, win_ref, wout_ref, nmem_ref, wkv_ref, pos_ref, freq_ref, onorm_ref,
         out_ref, z_ref, cat_ref, st_ref, mkT_ref, mv_ref, cs_ref, dec_ref, qdec_ref, kdec_ref) = refs

    @pl.when(pl.program_id(1) == 0)
    def _start_of_sequence():
        st_ref[...] = jnp.zeros_like(st_ref)
        _memory_kv(mem_ref, nmem_ref, wkv_ref, mkT_ref, mv_ref)
        if kind != "hgrn":
            _retention_tables(dec_ref, qdec_ref, kdec_ref)

    block = HGRN_BLOCK if kind == "hgrn" else RET_BLOCK
    if kind != "hgrn":
        ang = pos_ref[0].astype(_F32) * freq_ref[...]
        lane = lax.broadcasted_iota(jnp.int32, ang.shape, 1)
        cs_ref[0] = jnp.cos(ang)
        cs_ref[1] = jnp.where(lane < HEAD_DIM // 2, -1.0, 1.0) * jnp.sin(ang)

    for c in range(SEQ_TILE // PROJ_CHUNK):
        crows = slice(c * PROJ_CHUNK, (c + 1) * PROJ_CHUNK)
        hn = _rmsnorm(x_ref[0, crows, :], nmix_ref[...]).astype(_BF16)
        z_ref[crows, :] = _dot(hn, win_ref[...])
        for i in range(c * PROJ_CHUNK // block, (c + 1) * PROJ_CHUNK // block):
            rows = slice(i * block, (i + 1) * block)
            if kind == "hgrn":
                _hgrn_rows(z_ref, rows, lb_ref, onorm_ref, st_ref, cat_ref)
            else:
                _retention_rows(z_ref, rows, cs_ref, onorm_ref, st_ref, cat_ref, dec_ref, qdec_ref, kdec_ref)
            _cross_attention_rows(z_ref, rows, mkT_ref, mv_ref, cat_ref)
        out_ref[0, crows, :] = x_ref[0, crows, :] + _dot(cat_ref[crows, :], wout_ref[...])


def _mixer_layer(kind, x, mem, norm_mix, w_in, w_out, norm_mem, w_mem_kv, extra):
    bsz, seq, _ = x.shape
    const2 = lambda b, t: (0, 0)
    in_specs = [
        pl.BlockSpec((1, SEQ_TILE, D_MODEL), lambda b, t: (b, t, 0)),
        pl.BlockSpec((1, N_MEM, D_MODEL), lambda b, t: (b, 0, 0)),
        pl.BlockSpec((1, D_MODEL), const2),
        pl.BlockSpec((D_MODEL, IN_WIDTH), const2),
        pl.BlockSpec((CAT_WIDTH, D_MODEL), const2),
        pl.BlockSpec((1, D_MODEL), const2),
        pl.BlockSpec((D_MODEL, 2 * XATTN_WIDTH), const2),
    ]
    scratch = [
        pltpu.VMEM((SEQ_TILE, IN_WIDTH), _F32),
        pltpu.VMEM((SEQ_TILE, CAT_WIDTH), _BF16),
        pltpu.VMEM((MIX_HEADS, HEAD_DIM, HEAD_DIM), _F32),
        pltpu.VMEM((N_XATTN_HEADS, XATTN_WIDTH, N_MEM), _BF16),
        pltpu.VMEM((N_XATTN_HEADS, N_MEM, XATTN_WIDTH), _BF16),
    ]
    if kind == "hgrn":
        in_specs += [pl.BlockSpec((1, MIX_WIDTH), const2), pl.BlockSpec((1, MIX_WIDTH), const2)]
    else:
        in_specs += [
            pl.BlockSpec((1, SEQ_TILE, 1), lambda b, t: (b, t, 0)),
            pl.BlockSpec((1, HEAD_DIM), const2),
            pl.BlockSpec((1, MIX_WIDTH), const2),
        ]
        scratch += [
            pltpu.VMEM((2, SEQ_TILE, HEAD_DIM), _F32),
            pltpu.VMEM((MIX_HEADS, RET_BLOCK, RET_BLOCK), _F32),
            pltpu.VMEM((MIX_HEADS, RET_BLOCK, HEAD_DIM), _F32),
            pltpu.VMEM((MIX_HEADS, RET_BLOCK, HEAD_DIM), _F32),
        ]
    return pl.pallas_call(
        functools.partial(_mixer_kernel, kind=kind),
        grid=(bsz, seq // SEQ_TILE),
        in_specs=in_specs,
        out_specs=pl.BlockSpec((1, SEQ_TILE, D_MODEL), lambda b, t: (b, t, 0)),
        out_shape=jax.ShapeDtypeStruct(x.shape, x.dtype),
        scratch_shapes=scratch,
        compiler_params=pltpu.CompilerParams(
            dimension_semantics=("arbitrary", "arbitrary"), vmem_limit_bytes=VMEM_LIMIT_BYTES),
        name=f"{kind}_mixer_layer",
    )(x, mem, norm_mix, w_in, w_out, norm_mem, w_mem_kv, *extra)


def _ffn_kernel(*refs, final):
    if final:
        x_ref, n_ref, win_ref, wout_ref, nf_ref, out_ref = refs
    else:
        x_ref, n_ref, win_ref, wout_ref, out_ref = refs
    x = x_ref[...]
    hn = _rmsnorm(x, n_ref[...]).astype(_BF16)
    acc = x
    for c in range(D_FF // FFN_CHUNK):
        g = _dot(hn, win_ref[:, c * FFN_CHUNK:(c + 1) * FFN_CHUNK])
        u = _dot(hn, win_ref[:, D_FF + c * FFN_CHUNK:D_FF + (c + 1) * FFN_CHUNK])
        a = (g * jax.nn.sigmoid(g) * u).astype(_BF16)
        acc = acc + _dot(a, wout_ref[c * FFN_CHUNK:(c + 1) * FFN_CHUNK, :])
    if final:
        acc = _rmsnorm(acc, nf_ref[...])
    out_ref[...] = acc


def _ffn_layer(x2d, norm_ffn, w_ffn_in, w_ffn_out, norm_final):
    final = norm_final is not None
    const = lambda i: (0, 0)
    in_specs = [
        pl.BlockSpec((FFN_TILE, D_MODEL), lambda i: (i, 0)),
        pl.BlockSpec((1, D_MODEL), const),
        pl.BlockSpec((D_MODEL, 2 * D_FF), const),
        pl.BlockSpec((D_FF, D_MODEL), const),
    ]
    args = [x2d, norm_ffn, w_ffn_in, w_ffn_out]
    if final:
        in_specs.append(pl.BlockSpec((1, D_MODEL), const))
        args.append(norm_final)
    return pl.pallas_call(
        functools.partial(_ffn_kernel, final=final),
        grid=(x2d.shape[0] // FFN_TILE,),
        in_specs=in_specs,
        out_specs=pl.BlockSpec((FFN_TILE, D_MODEL), lambda i: (i, 0)),
        out_shape=jax.ShapeDtypeStruct(x2d.shape, x2d.dtype),
        compiler_params=pltpu.CompilerParams(
            dimension_semantics=("arbitrary",), vmem_limit_bytes=VMEM_LIMIT_BYTES),
        name="swiglu_ffn_final" if final else "swiglu_ffn",
    )(*args)


def kernel(x, mem, positions, norm_mix, w_in, w_out, norm_mem, w_mem_kv, hgrn_lb_logits, hgrn_out_norm,
           ret_out_norm, norm_ffn, w_ffn_in, w_ffn_out, norm_final):
    bsz, seq, _ = x.shape
    depth = w_in.shape[0]
    assert seq % SEQ_TILE == 0 and (bsz * seq) % FFN_TILE == 0

    p_lb = jax.nn.softmax(hgrn_lb_logits.astype(_F32), axis=0)
    lower_bounds = jnp.cumsum(p_lb, axis=0) - p_lb[0]
    half = HEAD_DIM // 2
    inv_freq = ROPE_BASE ** (-jnp.linspace(0.0, 1.0, half, dtype=_F32))
    inv_freq2 = jnp.concatenate([inv_freq, inv_freq]).reshape(1, HEAD_DIM)
    pos3 = positions.reshape(bsz, seq, 1)

    for i in range(depth):
        j = i // N_MIXERS
        common = (x, mem, norm_mix[i].reshape(1, D_MODEL), w_in[i].astype(_BF16), w_out[i].astype(_BF16),
                  norm_mem[i].reshape(1, D_MODEL), w_mem_kv[i].astype(_BF16))
        if i % N_MIXERS == 0:
            extra = (lower_bounds[j].reshape(1, MIX_WIDTH), hgrn_out_norm[j].reshape(1, MIX_WIDTH))
            x = _mixer_layer("hgrn", *common, extra)
        else:
            extra = (pos3, inv_freq2, ret_out_norm[j].reshape(1, MIX_WIDTH))
            x = _mixer_layer("retention", *common, extra)
        x = _ffn_layer(
            x.reshape(bsz * seq, D_MODEL), norm_ffn[i].reshape(1, D_MODEL), w_ffn_in[i].astype(_BF16),
            w_ffn_out[i].astype(_BF16), norm_final.reshape(1, D_MODEL) if i == depth - 1 else None,
        ).reshape(bsz, seq, D_MODEL)
    return x
```

```python
import functools
import math

import jax
import jax.numpy as jnp
from jax import lax
from jax.experimental import pallas as pl
from jax.experimental.pallas import tpu as pltpu

D_MODEL = 1024
MIX_HEADS = 6
HEAD_DIM = 128
MIX_WIDTH = MIX_HEADS * HEAD_DIM
N_XATTN_HEADS = 4
XATTN_HEAD_DIM = 64
XATTN_WIDTH = N_XATTN_HEADS * XATTN_HEAD_DIM
N_MEM = 256
IN_WIDTH = 4 * MIX_WIDTH + XATTN_WIDTH
CAT_WIDTH = MIX_WIDTH + XATTN_WIDTH
D_FF = 2816
N_MIXERS = 2
ROPE_BASE = 10000.0
EPS = 1e-6
EXP_CLAMP = 30.0

SEQ_TILE = 512
CHUNK = 256
PROJ_COLS = 256
HGRN_BLOCK = 128
HGRN_SUB = 32
N_SUB = HGRN_BLOCK // HGRN_SUB
RET_BLOCK = CHUNK
FFN_TILE = 512
FFN_CHUNK = 1408
VMEM_LIMIT_BYTES = 60 * 1024 * 1024

_BF16 = jnp.bfloat16
_F32 = jnp.float32


def _dot(a, b):
    return jnp.dot(a, b, preferred_element_type=_F32)


def _dot_nt(a, b):
    return lax.dot_general(a, b, (((1,), (1,)), ((), ())), preferred_element_type=_F32)


def _dot_tn(a, b):
    return lax.dot_general(a, b, (((0,), (0,)), ((), ())), preferred_element_type=_F32)


def _group_index(idx, group):
    shift = group.bit_length() - 1
    assert 1 << shift == group
    return lax.shift_right_logical(idx, jnp.int32(shift))


def _rmsnorm(x, w):
    return x * lax.rsqrt(jnp.mean(x * x, axis=-1, keepdims=True) + EPS) * w


def _head_lanes(h):
    return slice(h * HEAD_DIM, (h + 1) * HEAD_DIM)


def _memory_kv(mem_ref, nmem_ref, wkv_ref, mkT_ref, mv_ref):
    memh = _rmsnorm(mem_ref[0], nmem_ref[...])
    mkv = _dot(memh, wkv_ref[...])
    mk = mkv[:, :XATTN_WIDTH] * (XATTN_HEAD_DIM ** -0.5)
    mv = mkv[:, XATTN_WIDTH:]
    lane_head = _group_index(lax.broadcasted_iota(jnp.int32, (N_MEM, XATTN_WIDTH), 1), XATTN_HEAD_DIM)
    for h in range(N_XATTN_HEADS):
        mkT_ref[h] = jnp.where(lane_head == h, mk, 0.0).T.astype(_BF16)
        mv_ref[h] = jnp.where(lane_head == h, mv, 0.0).astype(_BF16)


def _cross_attention_items(z_ref, rows, mkT_ref, mv_ref, cat_ref):
    ctx = {}

    def head(h):
        def run():
            if h == 0:
                ctx["q"] = z_ref[rows, 4 * MIX_WIDTH:].astype(_BF16)
            s = _dot(ctx["q"], mkT_ref[h])
            e = jnp.exp(s - jnp.max(s, axis=-1, keepdims=True))
            p = (e / jnp.sum(e, axis=-1, keepdims=True)).astype(_BF16)
            o = _dot(p, mv_ref[h])
            ctx["acc"] = o if h == 0 else ctx["acc"] + o
            if h == N_XATTN_HEADS - 1:
                cat_ref[rows, MIX_WIDTH:] = ctx["acc"]
        return run

    return [head(h) for h in range(N_XATTN_HEADS)]


def _sub_rows(a, c):
    return a[c * HGRN_SUB:(c + 1) * HGRN_SUB]


def _scale_subs(a, vecs):
    parts = []
    for c, vec in enumerate(vecs):
        part = jnp.zeros((HGRN_SUB, a.shape[1]), _F32) if vec is None else _sub_rows(a, c) * vec
        parts.append(part.astype(_BF16))
    return jnp.concatenate(parts, axis=0)


def _hgrn_items(z_ref, rows, lb_ref, onorm_ref, st_ref, cat_ref):
    n = HGRN_BLOCK
    exp_clamp = math.exp(EXP_CLAMP)
    ops = {}
    outs = {}

    def masks():
        ti = lax.broadcasted_iota(jnp.int32, (n, n), 0)
        si = lax.broadcasted_iota(jnp.int32, (n, n), 1)
        mask_a = (_group_index(ti, HGRN_SUB) == _group_index(si, HGRN_SUB)) & (si <= ti)
        mask_b = _group_index(ti, 2 * HGRN_SUB) == _group_index(si, 2 * HGRN_SUB)
        return mask_a, mask_b, si <= ti

    def gates(h):
        def run():
            hs = _head_lanes(h)
            za = z_ref[rows, hs]
            fr = z_ref[rows, MIX_WIDTH + h * HEAD_DIM:MIX_WIDTH + (h + 1) * HEAD_DIM]
            lb = lb_ref[:, hs]
            q = za / (1.0 + jnp.exp(-za))
            e = jnp.exp(-jnp.abs(fr))
            r = 1.0 / (1.0 + e)
            nonneg = fr >= 0.0
            k = (1.0 - lb) * (jnp.where(nonneg, e, 1.0) * r)
            e_neg = jnp.where(nonneg, e, jnp.minimum(1.0 / e, exp_clamp))
            g = jnp.minimum(fr, 0.0) + jnp.log((1.0 + lb * e_neg) * r)
            g_hi = g.astype(_BF16)
            g_lo = (g - g_hi.astype(_F32)).astype(_BF16)
            lower_ones = jnp.where(masks()[2], 1.0, 0.0).astype(_BF16)
            bb = _dot(lower_ones, jnp.concatenate([g_hi, g_lo], axis=1))
            b = bb[:, :HEAD_DIM] + bb[:, HEAD_DIM:]

            mids = [b[c * HGRN_SUB + HGRN_SUB // 2 - 1:c * HGRN_SUB + HGRN_SUB // 2, :] for c in range(N_SUB)]
            ends = [b[(c + 1) * HGRN_SUB - 1:(c + 1) * HGRN_SUB, :] for c in range(N_SUB)]
            ref = jnp.concatenate([jnp.broadcast_to(m, (HGRN_SUB, HEAD_DIM)) for m in mids], axis=0)
            q_a = q * jnp.exp(b - ref)
            k_a = k * jnp.exp(ref - b)
            ops[h] = dict(
                q_a=q_a.astype(_BF16), k_a=k_a.astype(_BF16),
                q_b=_scale_subs(q_a, [None, jnp.exp(mids[1] - ends[0]), None, jnp.exp(mids[3] - ends[2])]),
                k_b=_scale_subs(k_a, [jnp.exp(ends[0] - mids[0]), None, jnp.exp(ends[2] - mids[2]), None]),
                q_c=_scale_subs(q_a, [None, None, jnp.exp(mids[2] - ends[1]), jnp.exp(mids[3] - ends[1])]),
                k_c=_scale_subs(k_a, [jnp.exp(ends[1] - mids[0]), jnp.exp(ends[1] - mids[1]), None, None]),
                q_o=_scale_subs(q_a, [jnp.exp(m) for m in mids]),
                k_s=_scale_subs(k_a, [jnp.exp(ends[N_SUB - 1] - m) for m in mids]),
                decay=jnp.exp(ends[N_SUB - 1]),
            )
        return run

    def mix(h):
        def run():
            hs = _head_lanes(h)
            op = ops.pop(h)
            mask_a, mask_b, _ = masks()
            vb = z_ref[rows, 2 * MIX_WIDTH + h * HEAD_DIM:2 * MIX_WIDTH + (h + 1) * HEAD_DIM].astype(_BF16)
            s_a = _dot_nt(op["q_a"], op["k_a"])
            s_b = _dot_nt(op["q_b"], op["k_b"])
            s_c = _dot_nt(op["q_c"], op["k_c"])
            scores = jnp.where(mask_a, s_a, 0.0) + jnp.where(mask_b, s_b, 0.0) + s_c
            st = st_ref[h]
            outs[h] = _dot(scores.astype(_BF16), vb) + _dot_nt(op["q_o"], st.astype(_BF16))
            st_ref[h] = st * op["decay"] + _dot_tn(vb, op["k_s"])
        return run

    def finish():
        ss = None
        for h in range(MIX_HEADS):
            part = jnp.sum(outs[h] * outs[h], axis=-1, keepdims=True)
            ss = part if ss is None else ss + part
        inv = lax.rsqrt(ss * (1.0 / MIX_WIDTH) + EPS)
        for h in range(MIX_HEADS):
            hs = _head_lanes(h)
            zg = z_ref[rows, 3 * MIX_WIDTH + h * HEAD_DIM:3 * MIX_WIDTH + (h + 1) * HEAD_DIM]
            cat_ref[rows, hs] = outs[h] * (inv * onorm_ref[:, hs]) / (1.0 + jnp.exp(-zg))

    heads = range(MIX_HEADS)
    return [gates(h) for h in heads] + [mix(h) for h in heads] + [finish]


def _retention_tables(dec_ref, qdec_ref, kdec_ref):
    n = RET_BLOCK
    scale = HEAD_DIM ** -0.5
    ti = lax.broadcasted_iota(jnp.int32, (n, n), 0)
    si = lax.broadcasted_iota(jnp.int32, (n, n), 1)
    rel = (ti - si).astype(_F32)
    pos = lax.broadcasted_iota(jnp.int32, (n, HEAD_DIM), 0).astype(_F32)
    for h in range(MIX_HEADS):
        lg = math.log(1.0 - 2.0 ** (-5.0 - h))
        dec_ref[h] = jnp.where(ti >= si, jnp.exp(lg * jnp.maximum(rel, 0.0)) * scale, 0.0)
        qdec_ref[h] = jnp.exp(lg * (pos + 1.0))
        kdec_ref[h] = jnp.exp(lg * (n - 1.0 - pos)) * scale


def _retention_items(z_ref, rows, cs_ref, onorm_ref, st_ref, cat_ref, dec_ref, qdec_ref, kdec_ref):
    n = RET_BLOCK

    def head(h):
        def run():
            cos2 = cs_ref[0]
            sin2 = cs_ref[1]

            def rope(t):
                return t * cos2 + pltpu.roll(t, HEAD_DIM // 2, 1) * sin2

            hs = _head_lanes(h)
            lg = math.log(1.0 - 2.0 ** (-5.0 - h))
            q = rope(z_ref[rows, hs])
            k = rope(z_ref[rows, MIX_WIDTH + h * HEAD_DIM:MIX_WIDTH + (h + 1) * HEAD_DIM])
            v = z_ref[rows, 2 * MIX_WIDTH + h * HEAD_DIM:2 * MIX_WIDTH + (h + 1) * HEAD_DIM]
            zg = z_ref[rows, 3 * MIX_WIDTH + h * HEAD_DIM:3 * MIX_WIDTH + (h + 1) * HEAD_DIM]
            qb = q.astype(_BF16)
            vb = v.astype(_BF16)
            scores = _dot_nt(qb, k.astype(_BF16)) * dec_ref[h]
            st = st_ref[h]
            o = _dot(scores.astype(_BF16), vb) + _dot_nt(qb, st.astype(_BF16)) * qdec_ref[h]
            k_s = k * kdec_ref[h]
            st_ref[h] = st * math.exp(lg * n) + _dot_tn(vb, k_s.astype(_BF16))
            inv = lax.rsqrt(jnp.mean(o * o, axis=-1, keepdims=True) + EPS)
            cat_ref[rows, hs] = o * (inv * onorm_ref[:, hs]) * (zg / (1.0 + jnp.exp(-zg)))
        return run

    return [head(h) for h in range(MIX_HEADS)]


def _rotary_item(pos_ref, crows, freq_ref, cs_ref):
    def run():
        ang = pos_ref[0, crows, :].astype(_F32) * freq_ref[...]
        lane = lax.broadcasted_iota(jnp.int32, ang.shape, 1)
        cs_ref[0] = jnp.cos(ang)
        cs_ref[1] = jnp.where(lane < HEAD_DIM // 2, -1.0, 1.0) * jnp.sin(ang)
    return run


def _in_proj_items(x_ref, crows, nmix_ref, win_ref, hn_ref, z_ref):
    def norm():
        hn_ref[...] = _rmsnorm(x_ref[0, crows, :], nmix_ref[...])

    def cols(p):
        def run():
            cs = slice(p * PROJ_COLS, (p + 1) * PROJ_COLS)
            z_ref[:, cs] = _dot(hn_ref[...], win_ref[:, cs])
        return run

    return [norm] + [cols(p) for p in range(IN_WIDTH // PROJ_COLS)]


def _out_proj_items(x_ref, crows, cat_ref, wout_ref, out_ref):
    def cols(p):
        def run():
            cs = slice(p * PROJ_COLS, (p + 1) * PROJ_COLS)
            out_ref[0, crows, cs] = x_ref[0, crows, cs] + _dot(cat_ref[...], wout_ref[:, cs])
        return run

    return [cols(p) for p in range(D_MODEL // PROJ_COLS)]


def _run_interleaved(main, filler):
    done = 0
    for k, item in enumerate(main):
        item()
        upto = (k + 1) * len(filler) // len(main)
        for f in filler[done:upto]:
            f()
        done = upto
    for f in filler[done:]:
        f()


def _mixer_kernel(*refs, kind):
    if kind == "hgrn":
        (x_ref, mem_ref, nmix_ref, win_ref, wout_ref, nmem_ref, wkv_ref, lb_ref, onorm_ref,
         out_ref, z_ref, hn_ref, cat_ref, st_ref, mkT_ref, mv_ref) = refs
    else:
        (x_ref, mem_ref, nmix_ref, win_ref, wout_ref, nmem_ref, wkv_ref, pos_ref, freq_ref, onorm_ref,
         out_ref, z_ref, hn_ref, cat_ref, st_ref, mkT_ref, mv_ref, cs_ref, dec_ref, qdec_ref, kdec_ref) = refs

    @pl.when(pl.program_id(1) == 0)
    def _start_of_sequence():
        st_ref[...] = jnp.zeros_like(st_ref)
        _memory_kv(mem_ref, nmem_ref, wkv_ref, mkT_ref, mv_ref)
        if kind != "hgrn":
            _retention_tables(dec_ref, qdec_ref, kdec_ref)

    n_chunks = SEQ_TILE // CHUNK

    def chunk_rows(c):
        return slice(c * CHUNK, (c + 1) * CHUNK)

    def stage_in(c):
        items = _in_proj_items(x_ref, chunk_rows(c), nmix_ref, win_ref, hn_ref, z_ref.at[c % 2])
        if kind != "hgrn":
            items.append(_rotary_item(pos_ref, chunk_rows(c), freq_ref, cs_ref.at[c % 2]))
        return items

    def stage_mix(c):
        zc, cc = z_ref.at[c % 2], cat_ref.at[c % 2]
        items = []
        if kind == "hgrn":
            for i in range(CHUNK // HGRN_BLOCK):
                rows = slice(i * HGRN_BLOCK, (i + 1) * HGRN_BLOCK)
                items += _hgrn_items(zc, rows, lb_ref, onorm_ref, st_ref, cc)
                items += _cross_attention_items(zc, rows, mkT_ref, mv_ref, cc)
        else:
            rows = slice(0, CHUNK)
            items += _retention_items(zc, rows, cs_ref.at[c % 2], onorm_ref, st_ref, cc, dec_ref, qdec_ref, kdec_ref)
            items += _cross_attention_items(zc, rows, mkT_ref, mv_ref, cc)
        return items

    def stage_out(c):
        return _out_proj_items(x_ref, chunk_rows(c), cat_ref.at[c % 2], wout_ref, out_ref)

    for item in stage_in(0):
        item()
    for c in range(n_chunks):
        filler = (stage_out(c - 1) if c > 0 else []) + (stage_in(c + 1) if c + 1 < n_chunks else [])
        _run_interleaved(stage_mix(c), filler)
    for item in stage_out(n_chunks - 1):
        item()


def _mixer_layer(kind, layer, x, mem, norm_mix, w_in, w_out, norm_mem, w_mem_kv, extra):
    bsz, seq, _ = x.shape
    const2 = lambda b, t: (0, 0)
    layer3 = lambda b, t: (layer, 0, 0)
    in_specs = [
        pl.BlockSpec((1, SEQ_TILE, D_MODEL), lambda b, t: (b, t, 0)),
        pl.BlockSpec((1, N_MEM, D_MODEL), lambda b, t: (b, 0, 0)),
        pl.BlockSpec((1, D_MODEL), const2),
        pl.BlockSpec((None, D_MODEL, IN_WIDTH), layer3, pipeline_mode=pl.Buffered(1)),
        pl.BlockSpec((None, CAT_WIDTH, D_MODEL), layer3, pipeline_mode=pl.Buffered(1)),
        pl.BlockSpec((1, D_MODEL), const2),
        pl.BlockSpec((None, D_MODEL, 2 * XATTN_WIDTH), layer3, pipeline_mode=pl.Buffered(1)),
    ]
    scratch = [
        pltpu.VMEM((2, CHUNK, IN_WIDTH), _F32),
        pltpu.VMEM((CHUNK, D_MODEL), _F32),
        pltpu.VMEM((2, CHUNK, CAT_WIDTH), _F32),
        pltpu.VMEM((MIX_HEADS, HEAD_DIM, HEAD_DIM), _F32),
        pltpu.VMEM((N_XATTN_HEADS, XATTN_WIDTH, N_MEM), _BF16),
        pltpu.VMEM((N_XATTN_HEADS, N_MEM, XATTN_WIDTH), _BF16),
    ]
    if kind == "hgrn":
        in_specs += [pl.BlockSpec((1, MIX_WIDTH), const2), pl.BlockSpec((1, MIX_WIDTH), const2)]
    else:
        in_specs += [
            pl.BlockSpec((1, SEQ_TILE, 1), lambda b, t: (b, t, 0)),
            pl.BlockSpec((1, HEAD_DIM), const2),
            pl.BlockSpec((1, MIX_WIDTH), const2),
        ]
        scratch += [
            pltpu.VMEM((2, 2, CHUNK, HEAD_DIM), _F32),
            pltpu.VMEM((MIX_HEADS, RET_BLOCK, RET_BLOCK), _F32),
            pltpu.VMEM((MIX_HEADS, RET_BLOCK, HEAD_DIM), _F32),
            pltpu.VMEM((MIX_HEADS, RET_BLOCK, HEAD_DIM), _F32),
        ]
    return pl.pallas_call(
        functools.partial(_mixer_kernel, kind=kind),
        grid=(bsz, seq // SEQ_TILE),
        in_specs=in_specs,
        out_specs=pl.BlockSpec((1, SEQ_TILE, D_MODEL), lambda b, t: (b, t, 0)),
        out_shape=jax.ShapeDtypeStruct(x.shape, x.dtype),
        scratch_shapes=scratch,
        compiler_params=pltpu.CompilerParams(
            dimension_semantics=("arbitrary", "arbitrary"), vmem_limit_bytes=VMEM_LIMIT_BYTES),
        name=f"{kind}_mixer_layer",
    )(x, mem, norm_mix, w_in, w_out, norm_mem, w_mem_kv, *extra)


def _ffn_kernel(*refs, final):
    if final:
        x_ref, n_ref, win_ref, wout_ref, nf_ref, out_ref = refs
    else:
        x_ref, n_ref, win_ref, wout_ref, out_ref = refs
    x = x_ref[...]
    hn = _rmsnorm(x, n_ref[...])
    acc = x
    for c in range(D_FF // FFN_CHUNK):
        g = _dot(hn, win_ref[:, c * FFN_CHUNK:(c + 1) * FFN_CHUNK])
        u = _dot(hn, win_ref[:, D_FF + c * FFN_CHUNK:D_FF + (c + 1) * FFN_CHUNK])
        a = g * jax.nn.sigmoid(g) * u
        acc = acc + _dot(a, wout_ref[c * FFN_CHUNK:(c + 1) * FFN_CHUNK, :])
    if final:
        acc = _rmsnorm(acc, nf_ref[...])
    out_ref[...] = acc


def _ffn_layer(x2d, layer, norm_ffn, w_ffn_in, w_ffn_out, norm_final):
    final = norm_final is not None
    const = lambda i: (0, 0)
    in_specs = [
        pl.BlockSpec((FFN_TILE, D_MODEL), lambda i: (i, 0)),
        pl.BlockSpec((1, D_MODEL), const),
        pl.BlockSpec((None, D_MODEL, 2 * D_FF), lambda i: (layer, 0, 0), pipeline_mode=pl.Buffered(1)),
        pl.BlockSpec((None, D_FF, D_MODEL), lambda i: (layer, 0, 0), pipeline_mode=pl.Buffered(1)),
    ]
    args = [x2d, norm_ffn, w_ffn_in, w_ffn_out]
    if final:
        in_specs.append(pl.BlockSpec((1, D_MODEL), const))
        args.append(norm_final)
    return pl.pallas_call(
        functools.partial(_ffn_kernel, final=final),
        grid=(x2d.shape[0] // FFN_TILE,),
        in_specs=in_specs,
        out_specs=pl.BlockSpec((FFN_TILE, D_MODEL), lambda i: (i, 0)),
        out_shape=jax.ShapeDtypeStruct(x2d.shape, x2d.dtype),
        compiler_params=pltpu.CompilerParams(
            dimension_semantics=("arbitrary",), vmem_limit_bytes=VMEM_LIMIT_BYTES),
        name="swiglu_ffn_final" if final else "swiglu_ffn",
    )(*args)


def kernel(x, mem, positions, norm_mix, w_in, w_out, norm_mem, w_mem_kv, hgrn_lb_logits, hgrn_out_norm,
           ret_out_norm, norm_ffn, w_ffn_in, w_ffn_out, norm_final):
    bsz, seq, _ = x.shape
    depth = w_in.shape[0]
    assert seq % SEQ_TILE == 0 and (bsz * seq) % FFN_TILE == 0

    p_lb = jax.nn.softmax(hgrn_lb_logits.astype(_F32), axis=0)
    lower_bounds = jnp.cumsum(p_lb, axis=0) - p_lb[0]
    half = HEAD_DIM // 2
    inv_freq = ROPE_BASE ** (-jnp.linspace(0.0, 1.0, half, dtype=_F32))
    inv_freq2 = jnp.concatenate([inv_freq, inv_freq]).reshape(1, HEAD_DIM)
    pos3 = positions.reshape(bsz, seq, 1)

    for i in range(depth):
        j = i // N_MIXERS
        common = (i, x, mem, norm_mix[i].reshape(1, D_MODEL), w_in, w_out, norm_mem[i].reshape(1, D_MODEL), w_mem_kv)
        if i % N_MIXERS == 0:
            extra = (lower_bounds[j].reshape(1, MIX_WIDTH), hgrn_out_norm[j].reshape(1, MIX_WIDTH))
            x = _mixer_layer("hgrn", *common, extra)
        else:
            extra = (pos3, inv_freq2, ret_out_norm[j].reshape(1, MIX_WIDTH))
            x = _mixer_layer("retention", *common, extra)
        x = _ffn_layer(
            x.reshape(bsz * seq, D_MODEL), i, norm_ffn[i].reshape(1, D_MODEL), w_ffn_in, w_ffn_out,
            norm_final.reshape(1, D_MODEL) if i == depth - 1 else None,
        ).reshape(bsz, seq, D_MODEL)
    return x
```

```python
import functools
import math

import jax
import jax.numpy as jnp
from jax import lax
from jax.experimental import pallas as pl
from jax.experimental.pallas import tpu as pltpu

D_MODEL = 1024
MIX_HEADS = 6
HEAD_DIM = 128
MIX_WIDTH = MIX_HEADS * HEAD_DIM
N_XATTN_HEADS = 4
XATTN_HEAD_DIM = 64
XATTN_WIDTH = N_XATTN_HEADS * XATTN_HEAD_DIM
N_MEM = 256
IN_WIDTH = 4 * MIX_WIDTH + XATTN_WIDTH
CAT_WIDTH = MIX_WIDTH + XATTN_WIDTH
D_FF = 2816
N_MIXERS = 2
ROPE_BASE = 10000.0
EPS = 1e-6
EXP_CLAMP = 30.0

SEQ_TILE = 512
CHUNK = 256
PROJ_COLS = 256
HGRN_BLOCK = 128
HGRN_SUB = 32
N_SUB = HGRN_BLOCK // HGRN_SUB
RET_BLOCK = CHUNK
FFN_TILE = 512
MXU_WIDTH = 256
FFN_CHUNKS = (6 * MXU_WIDTH, 5 * MXU_WIDTH)
VMEM_LIMIT_BYTES = 60 * 1024 * 1024

_BF16 = jnp.bfloat16
_F32 = jnp.float32


def _dot(a, b):
    return jnp.dot(a, b, preferred_element_type=_F32)


def _dot_nt(a, b):
    return lax.dot_general(a, b, (((1,), (1,)), ((), ())), preferred_element_type=_F32)


def _dot_tn(a, b):
    return lax.dot_general(a, b, (((0,), (0,)), ((), ())), preferred_element_type=_F32)


def _group_index(idx, group):
    shift = group.bit_length() - 1
    assert 1 << shift == group
    return lax.shift_right_logical(idx, jnp.int32(shift))


def _rmsnorm(x, w):
    return x * lax.rsqrt(jnp.mean(x * x, axis=-1, keepdims=True) + EPS) * w


def _head_lanes(h):
    return slice(h * HEAD_DIM, (h + 1) * HEAD_DIM)


def _log_gamma(h):
    return math.log(1.0 - 2.0 ** (-5.0 - h))


def _memory_kv(mem_ref, nmem_ref, wkv_ref, mkT_ref, mv_ref):
    memh = _rmsnorm(mem_ref[0], nmem_ref[...])
    mkv = _dot(memh, wkv_ref[...])
    mk = mkv[:, :XATTN_WIDTH] * (XATTN_HEAD_DIM ** -0.5)
    mv = mkv[:, XATTN_WIDTH:]
    lane_head = _group_index(lax.broadcasted_iota(jnp.int32, (N_MEM, XATTN_WIDTH), 1), XATTN_HEAD_DIM)
    for h in range(N_XATTN_HEADS):
        mkT_ref[h] = jnp.where(lane_head == h, mk, 0.0).T.astype(_BF16)
        mv_ref[h] = jnp.where(lane_head == h, mv, 0.0).astype(_BF16)


def _cross_attention_items(z_ref, rows, mkT_ref, mv_ref, cat_ref):
    ctx = {}

    def head(h):
        def run():
            if h == 0:
                ctx["q"] = z_ref[rows, 4 * MIX_WIDTH:].astype(_BF16)
            s = _dot(ctx["q"], mkT_ref[h])
            e = jnp.exp(s - jnp.max(s, axis=-1, keepdims=True))
            p = (e / jnp.sum(e, axis=-1, keepdims=True)).astype(_BF16)
            o = _dot(p, mv_ref[h])
            ctx["acc"] = o if h == 0 else ctx["acc"] + o
            if h == N_XATTN_HEADS - 1:
                cat_ref[rows, MIX_WIDTH:] = ctx["acc"]
        return run

    return [head(h) for h in range(N_XATTN_HEADS)]


def _sub_rows(a, c):
    return a[c * HGRN_SUB:(c + 1) * HGRN_SUB]


def _scale_subs(a, vecs):
    parts = []
    for c, vec in enumerate(vecs):
        part = jnp.zeros((HGRN_SUB, a.shape[1]), _F32) if vec is None else _sub_rows(a, c) * vec
        parts.append(part.astype(_BF16))
    return jnp.concatenate(parts, axis=0)


def _hgrn_items(z_ref, rows, lb_ref, onorm_ref, st_ref, cat_ref):
    n = HGRN_BLOCK
    exp_clamp = math.exp(EXP_CLAMP)
    ops = {}
    outs = {}

    def masks():
        ti = lax.broadcasted_iota(jnp.int32, (n, n), 0)
        si = lax.broadcasted_iota(jnp.int32, (n, n), 1)
        mask_a = (_group_index(ti, HGRN_SUB) == _group_index(si, HGRN_SUB)) & (si <= ti)
        mask_b = _group_index(ti, 2 * HGRN_SUB) == _group_index(si, 2 * HGRN_SUB)
        return mask_a, mask_b, si <= ti

    def gates(h):
        def run():
            hs = _head_lanes(h)
            za = z_ref[rows, hs]
            fr = z_ref[rows, MIX_WIDTH + h * HEAD_DIM:MIX_WIDTH + (h + 1) * HEAD_DIM]
            lb = lb_ref[:, hs]
            q = za / (1.0 + jnp.exp(-za))
            e = jnp.exp(-jnp.abs(fr))
            r = 1.0 / (1.0 + e)
            nonneg = fr >= 0.0
            k = (1.0 - lb) * (jnp.where(nonneg, e, 1.0) * r)
            e_neg = jnp.where(nonneg, e, jnp.minimum(1.0 / e, exp_clamp))
            g = jnp.minimum(fr, 0.0) + jnp.log((1.0 + lb * e_neg) * r)
            g_hi = g.astype(_BF16)
            g_lo = (g - g_hi.astype(_F32)).astype(_BF16)
            lower_ones = jnp.where(masks()[2], 1.0, 0.0).astype(_BF16)
            bb = _dot(lower_ones, jnp.concatenate([g_hi, g_lo], axis=1))
            b = bb[:, :HEAD_DIM] + bb[:, HEAD_DIM:]

            mids = [b[c * HGRN_SUB + HGRN_SUB // 2 - 1:c * HGRN_SUB + HGRN_SUB // 2, :] for c in range(N_SUB)]
            ends = [b[(c + 1) * HGRN_SUB - 1:(c + 1) * HGRN_SUB, :] for c in range(N_SUB)]
            ref = jnp.concatenate([jnp.broadcast_to(m, (HGRN_SUB, HEAD_DIM)) for m in mids], axis=0)
            q_a = q * jnp.exp(b - ref)
            k_a = k * jnp.exp(ref - b)
            ops[h] = dict(
                q_a=q_a.astype(_BF16), k_a=k_a.astype(_BF16),
                q_b=_scale_subs(q_a, [None, jnp.exp(mids[1] - ends[0]), None, jnp.exp(mids[3] - ends[2])]),
                k_b=_scale_subs(k_a, [jnp.exp(ends[0] - mids[0]), None, jnp.exp(ends[2] - mids[2]), None]),
                q_c=_scale_subs(q_a, [None, None, jnp.exp(mids[2] - ends[1]), jnp.exp(mids[3] - ends[1])]),
                k_c=_scale_subs(k_a, [jnp.exp(ends[1] - mids[0]), jnp.exp(ends[1] - mids[1]), None, None]),
                q_o=_scale_subs(q_a, [jnp.exp(m) for m in mids]),
                k_s=_scale_subs(k_a, [jnp.exp(ends[N_SUB - 1] - m) for m in mids]),
                decay=jnp.exp(ends[N_SUB - 1]),
            )
        return run

    def mix(h):
        def run():
            op = ops.pop(h)
            mask_a, mask_b, _ = masks()
            vb = z_ref[rows, 2 * MIX_WIDTH + h * HEAD_DIM:2 * MIX_WIDTH + (h + 1) * HEAD_DIM].astype(_BF16)
            s_a = _dot_nt(op["q_a"], op["k_a"])
            s_b = _dot_nt(op["q_b"], op["k_b"])
            s_c = _dot_nt(op["q_c"], op["k_c"])
            scores = jnp.where(mask_a, s_a, 0.0) + jnp.where(mask_b, s_b, 0.0) + s_c
            st = st_ref[h]
            outs[h] = _dot(scores.astype(_BF16), vb) + _dot_nt(op["q_o"], st.astype(_BF16))
            st_ref[h] = st * op["decay"] + _dot_tn(vb, op["k_s"])
        return run

    def finish():
        ss = None
        for h in range(MIX_HEADS):
            part = jnp.sum(outs[h] * outs[h], axis=-1, keepdims=True)
            ss = part if ss is None else ss + part
        inv = lax.rsqrt(ss * (1.0 / MIX_WIDTH) + EPS)
        for h in range(MIX_HEADS):
            hs = _head_lanes(h)
            zg = z_ref[rows, 3 * MIX_WIDTH + h * HEAD_DIM:3 * MIX_WIDTH + (h + 1) * HEAD_DIM]
            cat_ref[rows, hs] = outs[h] * (inv * onorm_ref[:, hs]) / (1.0 + jnp.exp(-zg))

    heads = range(MIX_HEADS)
    return [gates(h) for h in heads] + [mix(h) for h in heads] + [finish]


def _retention_tables(dec_ref, qdec_ref, kdec_ref):
    n = RET_BLOCK
    scale = HEAD_DIM ** -0.5
    ti = lax.broadcasted_iota(jnp.int32, (n, n), 0)
    si = lax.broadcasted_iota(jnp.int32, (n, n), 1)
    rel = (ti - si).astype(_F32)
    pos = lax.broadcasted_iota(jnp.int32, (n, HEAD_DIM), 0).astype(_F32)
    for h in range(MIX_HEADS):
        lg = math.log(1.0 - 2.0 ** (-5.0 - h))
        dec_ref[h] = jnp.where(ti >= si, jnp.exp(lg * jnp.maximum(rel, 0.0)) * scale, 0.0)
        qdec_ref[h] = jnp.exp(lg * (pos + 1.0))
        kdec_ref[h] = jnp.exp(lg * (n - 1.0 - pos)) * scale


def _retention_items(z_ref, rows, cs_ref, onorm_ref, st_ref, cat_ref, dec_ref, qdec_ref, kdec_ref):
    n = RET_BLOCK

    def head(h):
        def run():
            cos2 = cs_ref[0]
            sin2 = cs_ref[1]

            def rope(t):
                return t * cos2 + pltpu.roll(t, HEAD_DIM // 2, 1) * sin2

            hs = _head_lanes(h)
            q = rope(z_ref[rows, hs])
            k = rope(z_ref[rows, MIX_WIDTH + h * HEAD_DIM:MIX_WIDTH + (h + 1) * HEAD_DIM])
            v = z_ref[rows, 2 * MIX_WIDTH + h * HEAD_DIM:2 * MIX_WIDTH + (h + 1) * HEAD_DIM]
            zg = z_ref[rows, 3 * MIX_WIDTH + h * HEAD_DIM:3 * MIX_WIDTH + (h + 1) * HEAD_DIM]
            qb = q.astype(_BF16)
            vb = v.astype(_BF16)
            scores = _dot_nt(qb, k.astype(_BF16)) * dec_ref[h]
            st = st_ref[h]
            o = _dot(scores.astype(_BF16), vb) + _dot_nt(qb, st.astype(_BF16)) * qdec_ref[h]
            k_s = k * kdec_ref[h]
            st_ref[h] = st * math.exp(_log_gamma(h) * n) + _dot_tn(vb, k_s.astype(_BF16))
            inv = lax.rsqrt(jnp.mean(o * o, axis=-1, keepdims=True) + EPS)
            cat_ref[rows, hs] = o * (inv * onorm_ref[:, hs]) * (zg / (1.0 + jnp.exp(-zg)))
        return run

    return [head(h) for h in range(MIX_HEADS)]


def _rotary_item(pos_ref, crows, freq_ref, cs_ref):
    def run():
        ang = pos_ref[0, crows, :].astype(_F32) * freq_ref[...]
        lane = lax.broadcasted_iota(jnp.int32, ang.shape, 1)
        cs_ref[0] = jnp.cos(ang)
        cs_ref[1] = jnp.where(lane < HEAD_DIM // 2, -1.0, 1.0) * jnp.sin(ang)
    return run


def _in_proj_items(x_ref, crows, nmix_ref, win_ref, hn_ref, z_ref):
    def norm():
        hn_ref[...] = _rmsnorm(x_ref[0, crows, :], nmix_ref[...])

    def cols(p):
        def run():
            cs = slice(p * PROJ_COLS, (p + 1) * PROJ_COLS)
            z_ref[:, cs] = _dot(hn_ref[...], win_ref[:, cs])
        return run

    return [norm] + [cols(p) for p in range(IN_WIDTH // PROJ_COLS)]


def _out_proj_items(x_ref, crows, cat_ref, wout_ref, out_ref):
    def cols(p):
        def run():
            cs = slice(p * PROJ_COLS, (p + 1) * PROJ_COLS)
            out_ref[0, crows, cs] = x_ref[0, crows, cs] + _dot(cat_ref[...], wout_ref[:, cs])
        return run

    return [cols(p) for p in range(D_MODEL // PROJ_COLS)]


def _run_interleaved(main, filler):
    done = 0
    for k, item in enumerate(main):
        item()
        upto = (k + 1) * len(filler) // len(main)
        for f in filler[done:upto]:
            f()
        done = upto
    for f in filler[done:]:
        f()


def _mixer_kernel(*refs, kind):
    if kind == "hgrn":
        (x_ref, mem_ref, nmix_ref, win_ref, wout_ref, nmem_ref, wkv_ref, lb_ref, onorm_ref,
         out_ref, z_ref, hn_ref, cat_ref, st_ref, mkT_ref, mv_ref) = refs
    else:
        (x_ref, mem_ref, nmix_ref, win_ref, wout_ref, nmem_ref, wkv_ref, pos_ref, freq_ref, onorm_ref,
         out_ref, z_ref, hn_ref, cat_ref, st_ref, mkT_ref, mv_ref, cs_ref, dec_ref, qdec_ref, kdec_ref) = refs

    @pl.when(pl.program_id(1) == 0)
    def _start_of_sequence():
        st_ref[...] = jnp.zeros_like(st_ref)
        _memory_kv(mem_ref, nmem_ref, wkv_ref, mkT_ref, mv_ref)
        if kind != "hgrn":
            _retention_tables(dec_ref, qdec_ref, kdec_ref)

    n_chunks = SEQ_TILE // CHUNK

    def chunk_rows(c):
        return slice(c * CHUNK, (c + 1) * CHUNK)

    def stage_in(c):
        return _in_proj_items(x_ref, chunk_rows(c), nmix_ref, win_ref, hn_ref, z_ref.at[c % 2])

    def stage_mix(c):
        zc, cc = z_ref.at[c % 2], cat_ref.at[c % 2]
        items = []
        if kind == "hgrn":
            for i in range(CHUNK // HGRN_BLOCK):
                rows = slice(i * HGRN_BLOCK, (i + 1) * HGRN_BLOCK)
                items += _hgrn_items(zc, rows, lb_ref, onorm_ref, st_ref, cc)
                items += _cross_attention_items(zc, rows, mkT_ref, mv_ref, cc)
        else:
            rows = slice(0, CHUNK)
            items += _retention_items(zc, rows, cs_ref.at[c], onorm_ref, st_ref, cc, dec_ref, qdec_ref, kdec_ref)
            items += _cross_attention_items(zc, rows, mkT_ref, mv_ref, cc)
        return items

    def stage_out(c):
        return _out_proj_items(x_ref, chunk_rows(c), cat_ref.at[c % 2], wout_ref, out_ref)

    if kind != "hgrn":
        for c in range(n_chunks):
            _rotary_item(pos_ref, chunk_rows(c), freq_ref, cs_ref.at[c])()
    for item in stage_in(0):
        item()
    for c in range(n_chunks):
        filler = (stage_out(c - 1) if c > 0 else []) + (stage_in(c + 1) if c + 1 < n_chunks else [])
        _run_interleaved(stage_mix(c), filler)
    for item in stage_out(n_chunks - 1):
        item()


def _mixer_layer(kind, layer, x, mem, norm_mix, w_in, w_out, norm_mem, w_mem_kv, extra):
    bsz, seq, _ = x.shape
    const2 = lambda b, t: (0, 0)
    layer3 = lambda b, t: (layer, 0, 0)
    in_specs = [
        pl.BlockSpec((1, SEQ_TILE, D_MODEL), lambda b, t: (b, t, 0)),
        pl.BlockSpec((1, N_MEM, D_MODEL), lambda b, t: (b, 0, 0)),
        pl.BlockSpec((1, D_MODEL), const2),
        pl.BlockSpec((None, D_MODEL, IN_WIDTH), layer3, pipeline_mode=pl.Buffered(1)),
        pl.BlockSpec((None, CAT_WIDTH, D_MODEL), layer3, pipeline_mode=pl.Buffered(1)),
        pl.BlockSpec((1, D_MODEL), const2),
        pl.BlockSpec((None, D_MODEL, 2 * XATTN_WIDTH), layer3, pipeline_mode=pl.Buffered(1)),
    ]
    scratch = [
        pltpu.VMEM((2, CHUNK, IN_WIDTH), _F32),
        pltpu.VMEM((CHUNK, D_MODEL), _F32),
        pltpu.VMEM((2, CHUNK, CAT_WIDTH), _F32),
        pltpu.VMEM((MIX_HEADS, HEAD_DIM, HEAD_DIM), _F32),
        pltpu.VMEM((N_XATTN_HEADS, XATTN_WIDTH, N_MEM), _BF16),
        pltpu.VMEM((N_XATTN_HEADS, N_MEM, XATTN_WIDTH), _BF16),
    ]
    if kind == "hgrn":
        in_specs += [pl.BlockSpec((1, MIX_WIDTH), const2), pl.BlockSpec((1, MIX_WIDTH), const2)]
    else:
        in_specs += [
            pl.BlockSpec((1, SEQ_TILE, 1), lambda b, t: (b, t, 0)),
            pl.BlockSpec((1, HEAD_DIM), const2),
            pl.BlockSpec((1, MIX_WIDTH), const2),
        ]
        scratch += [
            pltpu.VMEM((SEQ_TILE // CHUNK, 2, CHUNK, HEAD_DIM), _F32),
            pltpu.VMEM((MIX_HEADS, RET_BLOCK, RET_BLOCK), _F32),
            pltpu.VMEM((MIX_HEADS, RET_BLOCK, HEAD_DIM), _F32),
            pltpu.VMEM((MIX_HEADS, RET_BLOCK, HEAD_DIM), _F32),
        ]
    return pl.pallas_call(
        functools.partial(_mixer_kernel, kind=kind),
        grid=(bsz, seq // SEQ_TILE),
        in_specs=in_specs,
        out_specs=pl.BlockSpec((1, SEQ_TILE, D_MODEL), lambda b, t: (b, t, 0)),
        out_shape=jax.ShapeDtypeStruct(x.shape, x.dtype),
        scratch_shapes=scratch,
        compiler_params=pltpu.CompilerParams(
            dimension_semantics=("arbitrary", "arbitrary"), vmem_limit_bytes=VMEM_LIMIT_BYTES),
        name=f"{kind}_mixer_layer",
    )(x, mem, norm_mix, w_in, w_out, norm_mem, w_mem_kv, *extra)


def _ffn_kernel(*refs, final):
    if final:
        x_ref, n_ref, win_ref, wout_ref, nf_ref, out_ref = refs
    else:
        x_ref, n_ref, win_ref, wout_ref, out_ref = refs
    x = x_ref[...]
    hn = _rmsnorm(x, n_ref[...])
    acc = x
    assert sum(FFN_CHUNKS) == D_FF
    lo = 0
    for width in FFN_CHUNKS:
        hi = lo + width
        g = _dot(hn, win_ref[:, lo:hi])
        u = _dot(hn, win_ref[:, D_FF + lo:D_FF + hi])
        a = g * jax.nn.sigmoid(g) * u
        acc = acc + _dot(a, wout_ref[lo:hi, :])
        lo = hi
    if final:
        acc = _rmsnorm(acc, nf_ref[...])
    out_ref[...] = acc


def _ffn_layer(x2d, layer, norm_ffn, w_ffn_in, w_ffn_out, norm_final):
    final = norm_final is not None
    const = lambda i: (0, 0)
    in_specs = [
        pl.BlockSpec((FFN_TILE, D_MODEL), lambda i: (i, 0)),
        pl.BlockSpec((1, D_MODEL), const),
        pl.BlockSpec((None, D_MODEL, 2 * D_FF), lambda i: (layer, 0, 0), pipeline_mode=pl.Buffered(1)),
        pl.BlockSpec((None, D_FF, D_MODEL), lambda i: (layer, 0, 0), pipeline_mode=pl.Buffered(1)),
    ]
    args = [x2d, norm_ffn, w_ffn_in, w_ffn_out]
    if final:
        in_specs.append(pl.BlockSpec((1, D_MODEL), const))
        args.append(norm_final)
    return pl.pallas_call(
        functools.partial(_ffn_kernel, final=final),
        grid=(x2d.shape[0] // FFN_TILE,),
        in_specs=in_specs,
        out_specs=pl.BlockSpec((FFN_TILE, D_MODEL), lambda i: (i, 0)),
        out_shape=jax.ShapeDtypeStruct(x2d.shape, x2d.dtype),
        compiler_params=pltpu.CompilerParams(
            dimension_semantics=("arbitrary",), vmem_limit_bytes=VMEM_LIMIT_BYTES),
        name="swiglu_ffn_final" if final else "swiglu_ffn",
    )(*args)


def kernel(x, mem, positions, norm_mix, w_in, w_out, norm_mem, w_mem_kv, hgrn_lb_logits, hgrn_out_norm,
           ret_out_norm, norm_ffn, w_ffn_in, w_ffn_out, norm_final):
    bsz, seq, _ = x.shape
    depth = w_in.shape[0]
    assert seq % SEQ_TILE == 0 and (bsz * seq) % FFN_TILE == 0

    p_lb = jax.nn.softmax(hgrn_lb_logits.astype(_F32), axis=0)
    lower_bounds = jnp.cumsum(p_lb, axis=0) - p_lb[0]
    half = HEAD_DIM // 2
    inv_freq = ROPE_BASE ** (-jnp.linspace(0.0, 1.0, half, dtype=_F32))
    inv_freq2 = jnp.concatenate([inv_freq, inv_freq]).reshape(1, HEAD_DIM)
    pos3 = positions.reshape(bsz, seq, 1)

    for i in range(depth):
        j = i // N_MIXERS
        common = (i, x, mem, norm_mix[i].reshape(1, D_MODEL), w_in, w_out, norm_mem[i].reshape(1, D_MODEL), w_mem_kv)
        if i % N_MIXERS == 0:
            extra = (lower_bounds[j].reshape(1, MIX_WIDTH), hgrn_out_norm[j].reshape(1, MIX_WIDTH))
            x = _mixer_layer("hgrn", *common, extra)
        else:
            extra = (pos3, inv_freq2, ret_out_norm[j].reshape(1, MIX_WIDTH))
            x = _mixer_layer("retention", *common, extra)
        x = _ffn_layer(
            x.reshape(bsz * seq, D_MODEL), i, norm_ffn[i].reshape(1, D_MODEL), w_ffn_in, w_ffn_out,
            norm_final.reshape(1, D_MODEL) if i == depth - 1 else None,
        ).reshape(bsz, seq, D_MODEL)
    return x
```

```python
import functools
import math

import jax
import jax.numpy as jnp
from jax import lax
from jax.experimental import pallas as pl
from jax.experimental.pallas import tpu as pltpu

D_MODEL = 1024
MIX_HEADS = 6
HEAD_DIM = 128
MIX_WIDTH = MIX_HEADS * HEAD_DIM
N_XATTN_HEADS = 4
XATTN_HEAD_DIM = 64
XATTN_WIDTH = N_XATTN_HEADS * XATTN_HEAD_DIM
N_MEM = 256
IN_WIDTH = 4 * MIX_WIDTH + XATTN_WIDTH
CAT_WIDTH = MIX_WIDTH + XATTN_WIDTH
D_FF = 2816
N_MIXERS = 2
ROPE_BASE = 10000.0
EPS = 1e-6
EXP_CLAMP = 30.0

SEQ_TILE = 512
CHUNK = {"hgrn": 256, "retention": 512}
PROJ_COLS = 256
HGRN_BLOCK = 128
HGRN_SUB = 32
N_SUB = HGRN_BLOCK // HGRN_SUB
RET_BLOCK = 256
FFN_TILE = 512
MXU_WIDTH = 256
FFN_CHUNKS = (6 * MXU_WIDTH, 5 * MXU_WIDTH)
VMEM_LIMIT_BYTES = 60 * 1024 * 1024

_BF16 = jnp.bfloat16
_F32 = jnp.float32


def _dot(a, b):
    return jnp.dot(a, b, preferred_element_type=_F32)


def _dot_nt(a, b):
    return lax.dot_general(a, b, (((1,), (1,)), ((), ())), preferred_element_type=_F32)


def _dot_tn(a, b):
    return lax.dot_general(a, b, (((0,), (0,)), ((), ())), preferred_element_type=_F32)


def _group_index(idx, group):
    shift = group.bit_length() - 1
    assert 1 << shift == group
    return lax.shift_right_logical(idx, jnp.int32(shift))


def _rmsnorm(x, w):
    return x * lax.rsqrt(jnp.mean(x * x, axis=-1, keepdims=True) + EPS) * w


def _head_lanes(h):
    return slice(h * HEAD_DIM, (h + 1) * HEAD_DIM)


def _log_gamma(h):
    return math.log(1.0 - 2.0 ** (-5.0 - h))


def _memory_kv(mem_ref, nmem_ref, wkv_ref, mkT_ref, mv_ref):
    memh = _rmsnorm(mem_ref[0], nmem_ref[...])
    mkv = _dot(memh, wkv_ref[...])
    mk = mkv[:, :XATTN_WIDTH] * (XATTN_HEAD_DIM ** -0.5)
    mv = mkv[:, XATTN_WIDTH:]
    lane_head = _group_index(lax.broadcasted_iota(jnp.int32, (N_MEM, XATTN_WIDTH), 1), XATTN_HEAD_DIM)
    for h in range(N_XATTN_HEADS):
        mkT_ref[h] = jnp.where(lane_head == h, mk, 0.0).T.astype(_BF16)
        mv_ref[h] = jnp.where(lane_head == h, mv, 0.0).astype(_BF16)


def _cross_attention_items(z_ref, rows, mkT_ref, mv_ref, cat_ref):
    ctx = {}

    def head(h):
        def run():
            if h == 0:
                ctx["q"] = z_ref[rows, 4 * MIX_WIDTH:].astype(_BF16)
            s = _dot(ctx["q"], mkT_ref[h])
            e = jnp.exp(s - jnp.max(s, axis=-1, keepdims=True))
            p = (e / jnp.sum(e, axis=-1, keepdims=True)).astype(_BF16)
            o = _dot(p, mv_ref[h])
            ctx["acc"] = o if h == 0 else ctx["acc"] + o
            if h == N_XATTN_HEADS - 1:
                cat_ref[rows, MIX_WIDTH:] = ctx["acc"]
        return run

    return [head(h) for h in range(N_XATTN_HEADS)]


def _sub_rows(a, c):
    return a[c * HGRN_SUB:(c + 1) * HGRN_SUB]


def _scale_subs(a, vecs):
    parts = []
    for c, vec in enumerate(vecs):
        part = jnp.zeros((HGRN_SUB, a.shape[1]), _F32) if vec is None else _sub_rows(a, c) * vec
        parts.append(part.astype(_BF16))
    return jnp.concatenate(parts, axis=0)


def _hgrn_items(z_ref, rows, lb_ref, onorm_ref, st_ref, cat_ref):
    n = HGRN_BLOCK
    exp_clamp = math.exp(EXP_CLAMP)
    ops = {}
    outs = {}

    def masks():
        ti = lax.broadcasted_iota(jnp.int32, (n, n), 0)
        si = lax.broadcasted_iota(jnp.int32, (n, n), 1)
        mask_a = (_group_index(ti, HGRN_SUB) == _group_index(si, HGRN_SUB)) & (si <= ti)
        mask_b = _group_index(ti, 2 * HGRN_SUB) == _group_index(si, 2 * HGRN_SUB)
        return mask_a, mask_b, si <= ti

    def gates(h):
        def run():
            hs = _head_lanes(h)
            za = z_ref[rows, hs]
            fr = z_ref[rows, MIX_WIDTH + h * HEAD_DIM:MIX_WIDTH + (h + 1) * HEAD_DIM]
            lb = lb_ref[:, hs]
            q = za / (1.0 + jnp.exp(-za))
            e = jnp.exp(-jnp.abs(fr))
            r = 1.0 / (1.0 + e)
            nonneg = fr >= 0.0
            k = (1.0 - lb) * (jnp.where(nonneg, e, 1.0) * r)
            e_neg = jnp.where(nonneg, e, jnp.minimum(1.0 / e, exp_clamp))
            g = jnp.minimum(fr, 0.0) + jnp.log((1.0 + lb * e_neg) * r)
            g_hi = g.astype(_BF16)
            g_lo = (g - g_hi.astype(_F32)).astype(_BF16)
            lower_ones = jnp.where(masks()[2], 1.0, 0.0).astype(_BF16)
            bb = _dot(lower_ones, jnp.concatenate([g_hi, g_lo], axis=1))
            b = bb[:, :HEAD_DIM] + bb[:, HEAD_DIM:]

            mids = [b[c * HGRN_SUB + HGRN_SUB // 2 - 1:c * HGRN_SUB + HGRN_SUB // 2, :] for c in range(N_SUB)]
            ends = [b[(c + 1) * HGRN_SUB - 1:(c + 1) * HGRN_SUB, :] for c in range(N_SUB)]
            ref = jnp.concatenate([jnp.broadcast_to(m, (HGRN_SUB, HEAD_DIM)) for m in mids], axis=0)
            q_a = q * jnp.exp(b - ref)
            k_a = k * jnp.exp(ref - b)
            ops[h] = dict(
                q_a=q_a.astype(_BF16), k_a=k_a.astype(_BF16),
                q_b=_scale_subs(q_a, [None, jnp.exp(mids[1] - ends[0]), None, jnp.exp(mids[3] - ends[2])]),
                k_b=_scale_subs(k_a, [jnp.exp(ends[0] - mids[0]), None, jnp.exp(ends[2] - mids[2]), None]),
                q_c=_scale_subs(q_a, [None, None, jnp.exp(mids[2] - ends[1]), jnp.exp(mids[3] - ends[1])]),
                k_c=_scale_subs(k_a, [jnp.exp(ends[1] - mids[0]), jnp.exp(ends[1] - mids[1]), None, None]),
                q_o=_scale_subs(q_a, [jnp.exp(m) for m in mids]),
                k_s=_scale_subs(k_a, [jnp.exp(ends[N_SUB - 1] - m) for m in mids]),
                decay=jnp.exp(ends[N_SUB - 1]),
            )
        return run

    def mix(h):
        def run():
            op = ops.pop(h)
            mask_a, mask_b, _ = masks()
            vb = z_ref[rows, 2 * MIX_WIDTH + h * HEAD_DIM:2 * MIX_WIDTH + (h + 1) * HEAD_DIM].astype(_BF16)
            s_a = _dot_nt(op["q_a"], op["k_a"])
            s_b = _dot_nt(op["q_b"], op["k_b"])
            s_c = _dot_nt(op["q_c"], op["k_c"])
            scores = jnp.where(mask_a, s_a, 0.0) + jnp.where(mask_b, s_b, 0.0) + s_c
            st = st_ref[h]
            outs[h] = _dot(scores.astype(_BF16), vb) + _dot_nt(op["q_o"], st.astype(_BF16))
            st_ref[h] = st * op["decay"] + _dot_tn(vb, op["k_s"])
        return run

    def finish():
        ss = None
        for h in range(MIX_HEADS):
            part = jnp.sum(outs[h] * outs[h], axis=-1, keepdims=True)
            ss = part if ss is None else ss + part
        inv = lax.rsqrt(ss * (1.0 / MIX_WIDTH) + EPS)
        for h in range(MIX_HEADS):
            hs = _head_lanes(h)
            zg = z_ref[rows, 3 * MIX_WIDTH + h * HEAD_DIM:3 * MIX_WIDTH + (h + 1) * HEAD_DIM]
            cat_ref[rows, hs] = outs[h] * (inv * onorm_ref[:, hs]) / (1.0 + jnp.exp(-zg))

    heads = range(MIX_HEADS)
    return [gates(h) for h in heads] + [mix(h) for h in heads] + [finish]


def _retention_tables(dec_ref, qdec_ref, kdec_ref):
    n = RET_BLOCK
    scale = HEAD_DIM ** -0.5
    ti = lax.broadcasted_iota(jnp.int32, (n, n), 0)
    si = lax.broadcasted_iota(jnp.int32, (n, n), 1)
    rel = (ti - si).astype(_F32)
    pos = lax.broadcasted_iota(jnp.int32, (n, HEAD_DIM), 0).astype(_F32)
    for h in range(MIX_HEADS):
        lg = math.log(1.0 - 2.0 ** (-5.0 - h))
        dec_ref[h] = jnp.where(ti >= si, jnp.exp(lg * jnp.maximum(rel, 0.0)) * scale, 0.0)
        qdec_ref[h] = jnp.exp(lg * (pos + 1.0))
        kdec_ref[h] = jnp.exp(lg * (n - 1.0 - pos)) * scale


def _retention_items(z_ref, rows, cs_ref, onorm_ref, st_ref, cat_ref, dec_ref, qdec_ref, kdec_ref):
    n = RET_BLOCK

    def head(h):
        def run():
            cos2 = cs_ref[0, rows, :]
            sin2 = cs_ref[1, rows, :]

            def rope(t):
                return t * cos2 + pltpu.roll(t, HEAD_DIM // 2, 1) * sin2

            hs = _head_lanes(h)
            q = rope(z_ref[rows, hs])
            k = rope(z_ref[rows, MIX_WIDTH + h * HEAD_DIM:MIX_WIDTH + (h + 1) * HEAD_DIM])
            v = z_ref[rows, 2 * MIX_WIDTH + h * HEAD_DIM:2 * MIX_WIDTH + (h + 1) * HEAD_DIM]
            zg = z_ref[rows, 3 * MIX_WIDTH + h * HEAD_DIM:3 * MIX_WIDTH + (h + 1) * HEAD_DIM]
            qb = q.astype(_BF16)
            vb = v.astype(_BF16)
            scores = _dot_nt(qb, k.astype(_BF16)) * dec_ref[h]
            st = st_ref[h]
            o = _dot(scores.astype(_BF16), vb) + _dot_nt(qb, st.astype(_BF16)) * qdec_ref[h]
            k_s = k * kdec_ref[h]
            st_ref[h] = st * math.exp(_log_gamma(h) * n) + _dot_tn(vb, k_s.astype(_BF16))
            inv = lax.rsqrt(jnp.mean(o * o, axis=-1, keepdims=True) + EPS)
            cat_ref[rows, hs] = o * (inv * onorm_ref[:, hs]) * (zg / (1.0 + jnp.exp(-zg)))
        return run

    return [head(h) for h in range(MIX_HEADS)]


def _rotary_item(pos_ref, crows, freq_ref, cs_ref):
    def run():
        ang = pos_ref[0, crows, :].astype(_F32) * freq_ref[...]
        lane = lax.broadcasted_iota(jnp.int32, ang.shape, 1)
        cs_ref[0] = jnp.cos(ang)
        cs_ref[1] = jnp.where(lane < HEAD_DIM // 2, -1.0, 1.0) * jnp.sin(ang)
    return run


def _in_proj_items(x_ref, crows, nmix_ref, win_ref, hn_ref, z_ref):
    def norm():
        hn_ref[...] = _rmsnorm(x_ref[0, crows, :], nmix_ref[...])

    def cols(p):
        def run():
            cs = slice(p * PROJ_COLS, (p + 1) * PROJ_COLS)
            z_ref[:, cs] = _dot(hn_ref[...], win_ref[:, cs])
        return run

    return [norm] + [cols(p) for p in range(IN_WIDTH // PROJ_COLS)]


def _out_proj_items(x_ref, crows, cat_ref, wout_ref, out_ref):
    def cols(p):
        def run():
            cs = slice(p * PROJ_COLS, (p + 1) * PROJ_COLS)
            out_ref[0, crows, cs] = x_ref[0, crows, cs] + _dot(cat_ref[...], wout_ref[:, cs])
        return run

    return [cols(p) for p in range(D_MODEL // PROJ_COLS)]


def _run_interleaved(main, filler):
    done = 0
    for k, item in enumerate(main):
        item()
        upto = (k + 1) * len(filler) // len(main)
        for f in filler[done:upto]:
            f()
        done = upto
    for f in filler[done:]:
        f()


def _mixer_kernel(*refs, kind):
    if kind == "hgrn":
        (x_ref, mem_ref, nmix_ref, win_ref, wout_ref, nmem_ref, wkv_ref, lb_ref, onorm_ref,
         out_ref, z_ref, hn_ref, cat_ref, st_ref, mkT_ref, mv_ref) = refs
    else:
        (x_ref, mem_ref, nmix_ref, win_ref, wout_ref, nmem_ref, wkv_ref, pos_ref, freq_ref, onorm_ref,
         out_ref, z_ref, hn_ref, cat_ref, st_ref, mkT_ref, mv_ref, cs_ref, dec_ref, qdec_ref, kdec_ref) = refs

    @pl.when(pl.program_id(1) == 0)
    def _start_of_sequence():
        st_ref[...] = jnp.zeros_like(st_ref)
        _memory_kv(mem_ref, nmem_ref, wkv_ref, mkT_ref, mv_ref)
        if kind != "hgrn":
            _retention_tables(dec_ref, qdec_ref, kdec_ref)

    chunk = CHUNK[kind]
    n_chunks = SEQ_TILE // chunk

    def chunk_rows(c):
        return slice(c * chunk, (c + 1) * chunk)

    def stage_in(c):
        return _in_proj_items(x_ref, chunk_rows(c), nmix_ref, win_ref, hn_ref, z_ref.at[c % 2])

    def stage_mix(c):
        zc, cc = z_ref.at[c % 2], cat_ref.at[c % 2]
        whole = slice(0, chunk)
        xattn = _cross_attention_items(zc, whole, mkT_ref, mv_ref, cc)
        if kind == "hgrn":
            items = xattn
            for i in range(chunk // HGRN_BLOCK):
                rows = slice(i * HGRN_BLOCK, (i + 1) * HGRN_BLOCK)
                items += _hgrn_items(zc, rows, lb_ref, onorm_ref, st_ref, cc)
            return items
        items = []
        for i in range(chunk // RET_BLOCK):
            rows = slice(i * RET_BLOCK, (i + 1) * RET_BLOCK)
            items += _retention_items(zc, rows, cs_ref.at[c], onorm_ref, st_ref, cc, dec_ref, qdec_ref, kdec_ref)
        return items + xattn

    def stage_out(c):
        return _out_proj_items(x_ref, chunk_rows(c), cat_ref.at[c % 2], wout_ref, out_ref)

    if kind != "hgrn":
        for c in range(n_chunks):
            _rotary_item(pos_ref, chunk_rows(c), freq_ref, cs_ref.at[c])()
    for item in stage_in(0):
        item()
    for c in range(n_chunks):
        filler = (stage_out(c - 1) if c > 0 else []) + (stage_in(c + 1) if c + 1 < n_chunks else [])
        _run_interleaved(stage_mix(c), filler)
    for item in stage_out(n_chunks - 1):
        item()


def _mixer_layer(kind, layer, x, mem, norm_mix, w_in, w_out, norm_mem, w_mem_kv, extra):
    bsz, seq, _ = x.shape
    const2 = lambda b, t: (0, 0)
    layer3 = lambda b, t: (layer, 0, 0)
    in_specs = [
        pl.BlockSpec((1, SEQ_TILE, D_MODEL), lambda b, t: (b, t, 0)),
        pl.BlockSpec((1, N_MEM, D_MODEL), lambda b, t: (b, 0, 0)),
        pl.BlockSpec((1, D_MODEL), const2),
        pl.BlockSpec((None, D_MODEL, IN_WIDTH), layer3, pipeline_mode=pl.Buffered(1)),
        pl.BlockSpec((None, CAT_WIDTH, D_MODEL), layer3, pipeline_mode=pl.Buffered(1)),
        pl.BlockSpec((1, D_MODEL), const2),
        pl.BlockSpec((None, D_MODEL, 2 * XATTN_WIDTH), layer3, pipeline_mode=pl.Buffered(1)),
    ]
    chunk = CHUNK[kind]
    n_slots = min(2, SEQ_TILE // chunk)
    scratch = [
        pltpu.VMEM((n_slots, chunk, IN_WIDTH), _F32),
        pltpu.VMEM((chunk, D_MODEL), _F32),
        pltpu.VMEM((n_slots, chunk, CAT_WIDTH), _F32),
        pltpu.VMEM((MIX_HEADS, HEAD_DIM, HEAD_DIM), _F32),
        pltpu.VMEM((N_XATTN_HEADS, XATTN_WIDTH, N_MEM), _BF16),
        pltpu.VMEM((N_XATTN_HEADS, N_MEM, XATTN_WIDTH), _BF16),
    ]
    if kind == "hgrn":
        in_specs += [pl.BlockSpec((1, MIX_WIDTH), const2), pl.BlockSpec((1, MIX_WIDTH), const2)]
    else:
        in_specs += [
            pl.BlockSpec((1, SEQ_TILE, 1), lambda b, t: (b, t, 0)),
            pl.BlockSpec((1, HEAD_DIM), const2),
            pl.BlockSpec((1, MIX_WIDTH), const2),
        ]
        scratch += [
            pltpu.VMEM((SEQ_TILE // chunk, 2, chunk, HEAD_DIM), _F32),
            pltpu.VMEM((MIX_HEADS, RET_BLOCK, RET_BLOCK), _F32),
            pltpu.VMEM((MIX_HEADS, RET_BLOCK, HEAD_DIM), _F32),
            pltpu.VMEM((MIX_HEADS, RET_BLOCK, HEAD_DIM), _F32),
        ]
    return pl.pallas_call(
        functools.partial(_mixer_kernel, kind=kind),
        grid=(bsz, seq // SEQ_TILE),
        in_specs=in_specs,
        out_specs=pl.BlockSpec((1, SEQ_TILE, D_MODEL), lambda b, t: (b, t, 0)),
        out_shape=jax.ShapeDtypeStruct(x.shape, x.dtype),
        scratch_shapes=scratch,
        compiler_params=pltpu.CompilerParams(
            dimension_semantics=("arbitrary", "arbitrary"), vmem_limit_bytes=VMEM_LIMIT_BYTES),
        name=f"{kind}_mixer_layer",
    )(x, mem, norm_mix, w_in, w_out, norm_mem, w_mem_kv, *extra)


def _ffn_kernel(*refs, final):
    if final:
        x_ref, n_ref, win_ref, wout_ref, nf_ref, out_ref = refs
    else:
        x_ref, n_ref, win_ref, wout_ref, out_ref = refs
    x = x_ref[...]
    hn = _rmsnorm(x, n_ref[...])
    acc = x
    assert sum(FFN_CHUNKS) == D_FF
    lo = 0
    for width in FFN_CHUNKS:
        hi = lo + width
        g = _dot(hn, win_ref[:, lo:hi])
        u = _dot(hn, win_ref[:, D_FF + lo:D_FF + hi])
        a = g * jax.nn.sigmoid(g) * u
        acc = acc + _dot(a, wout_ref[lo:hi, :])
        lo = hi
    if final:
        acc = _rmsnorm(acc, nf_ref[...])
    out_ref[...] = acc


def _ffn_layer(x2d, layer, norm_ffn, w_ffn_in, w_ffn_out, norm_final):
    final = norm_final is not None
    const = lambda i: (0, 0)
    in_specs = [
        pl.BlockSpec((FFN_TILE, D_MODEL), lambda i: (i, 0)),
        pl.BlockSpec((1, D_MODEL), const),
        pl.BlockSpec((None, D_MODEL, 2 * D_FF), lambda i: (layer, 0, 0), pipeline_mode=pl.Buffered(1)),
        pl.BlockSpec((None, D_FF, D_MODEL), lambda i: (layer, 0, 0), pipeline_mode=pl.Buffered(1)),
    ]
    args = [x2d, norm_ffn, w_ffn_in, w_ffn_out]
    if final:
        in_specs.append(pl.BlockSpec((1, D_MODEL), const))
        args.append(norm_final)
    return pl.pallas_call(
        functools.partial(_ffn_kernel, final=final),
        grid=(x2d.shape[0] // FFN_TILE,),
        in_specs=in_specs,
        out_specs=pl.BlockSpec((FFN_TILE, D_MODEL), lambda i: (i, 0)),
        out_shape=jax.ShapeDtypeStruct(x2d.shape, x2d.dtype),
        compiler_params=pltpu.CompilerParams(
            dimension_semantics=("arbitrary",), vmem_limit_bytes=VMEM_LIMIT_BYTES),
        name="swiglu_ffn_final" if final else "swiglu_ffn",
    )(*args)


def kernel(x, mem, positions, norm_mix, w_in, w_out, norm_mem, w_mem_kv, hgrn_lb_logits, hgrn_out_norm,
           ret_out_norm, norm_ffn, w_ffn_in, w_ffn_out, norm_final):
    bsz, seq, _ = x.shape
    depth = w_in.shape[0]
    assert seq % SEQ_TILE == 0 and (bsz * seq) % FFN_TILE == 0

    p_lb = jax.nn.softmax(hgrn_lb_logits.astype(_F32), axis=0)
    lower_bounds = jnp.cumsum(p_lb, axis=0) - p_lb[0]
    half = HEAD_DIM // 2
    inv_freq = ROPE_BASE ** (-jnp.linspace(0.0, 1.0, half, dtype=_F32))
    inv_freq2 = jnp.concatenate([inv_freq, inv_freq]).reshape(1, HEAD_DIM)
    pos3 = positions.reshape(bsz, seq, 1)

    for i in range(depth):
        j = i // N_MIXERS
        common = (i, x, mem, norm_mix[i].reshape(1, D_MODEL), w_in, w_out, norm_mem[i].reshape(1, D_MODEL), w_mem_kv)
        if i % N_MIXERS == 0:
            extra = (lower_bounds[j].reshape(1, MIX_WIDTH), hgrn_out_norm[j].reshape(1, MIX_WIDTH))
            x = _mixer_layer("hgrn", *common, extra)
        else:
            extra = (pos3, inv_freq2, ret_out_norm[j].reshape(1, MIX_WIDTH))
            x = _mixer_layer("retention", *common, extra)
        x = _ffn_layer(
            x.reshape(bsz * seq, D_MODEL), i, norm_ffn[i].reshape(1, D_MODEL), w_ffn_in, w_ffn_out,
            norm_final.reshape(1, D_MODEL) if i == depth - 1 else None,
        ).reshape(bsz, seq, D_MODEL)
    return x
```

```python
import functools
import math

import jax
import jax.numpy as jnp
from jax import lax
from jax.experimental import pallas as pl
from jax.experimental.pallas import tpu as pltpu

D_MODEL = 1024
MIX_HEADS = 6
HEAD_DIM = 128
MIX_WIDTH = MIX_HEADS * HEAD_DIM
N_XATTN_HEADS = 4
XATTN_HEAD_DIM = 64
XATTN_WIDTH = N_XATTN_HEADS * XATTN_HEAD_DIM
N_MEM = 256
IN_WIDTH = 4 * MIX_WIDTH + XATTN_WIDTH
CAT_WIDTH = MIX_WIDTH + XATTN_WIDTH
D_FF = 2816
N_MIXERS = 2
ROPE_BASE = 10000.0
EPS = 1e-6
EXP_CLAMP = 30.0

SEQ_TILE = 512
CHUNK = {"hgrn": 256, "retention": 512}
PROJ_COLS = 256
HGRN_BLOCK = 128
HGRN_SUB = 32
N_SUB = HGRN_BLOCK // HGRN_SUB
RET_BLOCK = 256
FFN_TILE = 512
MXU_WIDTH = 256
FFN_CHUNKS = (6 * MXU_WIDTH, 5 * MXU_WIDTH)
VMEM_LIMIT_BYTES = 60 * 1024 * 1024

_BF16 = jnp.bfloat16
_F32 = jnp.float32


def _dot(a, b):
    return jnp.dot(a, b, preferred_element_type=_F32)


def _dot_nt(a, b):
    return lax.dot_general(a, b, (((1,), (1,)), ((), ())), preferred_element_type=_F32)


def _dot_tn(a, b):
    return lax.dot_general(a, b, (((0,), (0,)), ((), ())), preferred_element_type=_F32)


def _group_index(idx, group):
    shift = group.bit_length() - 1
    assert 1 << shift == group
    return lax.shift_right_logical(idx, jnp.int32(shift))


def _rmsnorm(x, w):
    return x * lax.rsqrt(jnp.mean(x * x, axis=-1, keepdims=True) + EPS) * w


def _head_lanes(h):
    return slice(h * HEAD_DIM, (h + 1) * HEAD_DIM)


def _log_gamma(h):
    return math.log(1.0 - 2.0 ** (-5.0 - h))


def _memory_kv(mem_ref, nmem_ref, wkv_ref, mkT_ref, mv_ref):
    memh = _rmsnorm(mem_ref[0], nmem_ref[...])
    mkv = _dot(memh, wkv_ref[...])
    mkT_ref[...] = (mkv[:, :XATTN_WIDTH] * (XATTN_HEAD_DIM ** -0.5)).T.astype(_BF16)
    mv = mkv[:, XATTN_WIDTH:]
    lane_head = _group_index(lax.broadcasted_iota(jnp.int32, (N_MEM, XATTN_WIDTH), 1), XATTN_HEAD_DIM)
    for h in range(N_XATTN_HEADS):
        mv_ref[h * N_MEM:(h + 1) * N_MEM, :] = jnp.where(lane_head == h, mv, 0.0).astype(_BF16)


def _cross_attention_items(z_ref, rows, mkT_ref, mv_ref, cat_ref):
    ctx = {}

    def scores():
        q = z_ref[rows, 4 * MIX_WIDTH:].astype(_BF16)
        lane_head = _group_index(lax.broadcasted_iota(jnp.int32, q.shape, 1), XATTN_HEAD_DIM)
        q_heads = [jnp.where(lane_head == h, q, jnp.zeros_like(q)) for h in range(N_XATTN_HEADS)]
        ctx["s"] = _dot(jnp.concatenate(q_heads, axis=0), mkT_ref[...])

    def softmax(h):
        def run():
            r = ctx["s"].shape[0] // N_XATTN_HEADS
            s = ctx["s"][h * r:(h + 1) * r]
            e = jnp.exp(s - jnp.max(s, axis=-1, keepdims=True))
            ctx[h] = (e / jnp.sum(e, axis=-1, keepdims=True)).astype(_BF16)
        return run

    def outputs():
        p = jnp.concatenate([ctx.pop(h) for h in range(N_XATTN_HEADS)], axis=1)
        cat_ref[rows, MIX_WIDTH:] = _dot(p, mv_ref[...])

    return [scores] + [softmax(h) for h in range(N_XATTN_HEADS)] + [outputs]


def _sub_rows(a, c):
    return a[c * HGRN_SUB:(c + 1) * HGRN_SUB]


def _scale_subs(a, vecs):
    parts = []
    for c, vec in enumerate(vecs):
        part = jnp.zeros((HGRN_SUB, a.shape[1]), _F32) if vec is None else _sub_rows(a, c) * vec
        parts.append(part.astype(_BF16))
    return jnp.concatenate(parts, axis=0)


def _hgrn_items(z_ref, rows, lb_ref, onorm_ref, st_ref, cat_ref):
    n = HGRN_BLOCK
    exp_clamp = math.exp(EXP_CLAMP)
    ops = {}
    outs = {}

    def masks():
        ti = lax.broadcasted_iota(jnp.int32, (n, n), 0)
        si = lax.broadcasted_iota(jnp.int32, (n, n), 1)
        mask_a = (_group_index(ti, HGRN_SUB) == _group_index(si, HGRN_SUB)) & (si <= ti)
        mask_b = _group_index(ti, 2 * HGRN_SUB) == _group_index(si, 2 * HGRN_SUB)
        return mask_a, mask_b, si <= ti

    def gates(h):
        def run():
            hs = _head_lanes(h)
            za = z_ref[rows, hs]
            fr = z_ref[rows, MIX_WIDTH + h * HEAD_DIM:MIX_WIDTH + (h + 1) * HEAD_DIM]
            lb = lb_ref[:, hs]
            q = za / (1.0 + jnp.exp(-za))
            e = jnp.exp(-jnp.abs(fr))
            r = 1.0 / (1.0 + e)
            nonneg = fr >= 0.0
            k = (1.0 - lb) * (jnp.where(nonneg, e, 1.0) * r)
            e_neg = jnp.where(nonneg, e, jnp.minimum(1.0 / e, exp_clamp))
            g = jnp.minimum(fr, 0.0) + jnp.log((1.0 + lb * e_neg) * r)
            g_hi = g.astype(_BF16)
            g_lo = (g - g_hi.astype(_F32)).astype(_BF16)
            lower_ones = jnp.where(masks()[2], 1.0, 0.0).astype(_BF16)
            bb = _dot(lower_ones, jnp.concatenate([g_hi, g_lo], axis=1))
            b = bb[:, :HEAD_DIM] + bb[:, HEAD_DIM:]

            mids = [b[c * HGRN_SUB + HGRN_SUB // 2 - 1:c * HGRN_SUB + HGRN_SUB // 2, :] for c in range(N_SUB)]
            ends = [b[(c + 1) * HGRN_SUB - 1:(c + 1) * HGRN_SUB, :] for c in range(N_SUB)]
            ref = jnp.concatenate([jnp.broadcast_to(m, (HGRN_SUB, HEAD_DIM)) for m in mids], axis=0)
            q_a = q * jnp.exp(b - ref)
            k_a = k * jnp.exp(ref - b)
            ops[h] = dict(
                q_a=q_a.astype(_BF16), k_a=k_a.astype(_BF16),
                q_b=_scale_subs(q_a, [None, jnp.exp(mids[1] - ends[0]), None, jnp.exp(mids[3] - ends[2])]),
                k_b=_scale_subs(k_a, [jnp.exp(ends[0] - mids[0]), None, jnp.exp(ends[2] - mids[2]), None]),
                q_c=_scale_subs(q_a, [None, None, jnp.exp(mids[2] - ends[1]), jnp.exp(mids[3] - ends[1])]),
                k_c=_scale_subs(k_a, [jnp.exp(ends[1] - mids[0]), jnp.exp(ends[1] - mids[1]), None, None]),
                q_o=_scale_subs(q_a, [jnp.exp(m) for m in mids]),
                k_s=_scale_subs(k_a, [jnp.exp(ends[N_SUB - 1] - m) for m in mids]),
                decay=jnp.exp(ends[N_SUB - 1]),
            )
        return run

    def mix(h):
        def run():
            op = ops.pop(h)
            mask_a, mask_b, _ = masks()
            vb = z_ref[rows, 2 * MIX_WIDTH + h * HEAD_DIM:2 * MIX_WIDTH + (h + 1) * HEAD_DIM].astype(_BF16)
            s_a = _dot_nt(op["q_a"], op["k_a"])
            s_b = _dot_nt(op["q_b"], op["k_b"])
            s_c = _dot_nt(op["q_c"], op["k_c"])
            scores = jnp.where(mask_a, s_a, 0.0) + jnp.where(mask_b, s_b, 0.0) + s_c
            st = st_ref[h]
            outs[h] = _dot(scores.astype(_BF16), vb) + _dot_nt(op["q_o"], st.astype(_BF16))
            st_ref[h] = st * op["decay"] + _dot_tn(vb, op["k_s"])
        return run

    def finish():
        ss = None
        for h in range(MIX_HEADS):
            part = jnp.sum(outs[h] * outs[h], axis=-1, keepdims=True)
            ss = part if ss is None else ss + part
        inv = lax.rsqrt(ss * (1.0 / MIX_WIDTH) + EPS)
        for h in range(MIX_HEADS):
            hs = _head_lanes(h)
            zg = z_ref[rows, 3 * MIX_WIDTH + h * HEAD_DIM:3 * MIX_WIDTH + (h + 1) * HEAD_DIM]
            cat_ref[rows, hs] = outs[h] * (inv * onorm_ref[:, hs]) / (1.0 + jnp.exp(-zg))

    heads = range(MIX_HEADS)
    return [gates(h) for h in heads] + [mix(h) for h in heads] + [finish]


def _retention_tables(dec_ref, qdec_ref, kdec_ref):
    n = RET_BLOCK
    scale = HEAD_DIM ** -0.5
    ti = lax.broadcasted_iota(jnp.int32, (n, n), 0)
    si = lax.broadcasted_iota(jnp.int32, (n, n), 1)
    rel = (ti - si).astype(_F32)
    pos = lax.broadcasted_iota(jnp.int32, (n, HEAD_DIM), 0).astype(_F32)
    for h in range(MIX_HEADS):
        lg = math.log(1.0 - 2.0 ** (-5.0 - h))
        dec_ref[h] = jnp.where(ti >= si, jnp.exp(lg * jnp.maximum(rel, 0.0)) * scale, 0.0)
        qdec_ref[h] = jnp.exp(lg * (pos + 1.0))
        kdec_ref[h] = jnp.exp(lg * (n - 1.0 - pos)) * scale


def _retention_items(z_ref, rows, cs_ref, onorm_ref, st_ref, cat_ref, dec_ref, qdec_ref, kdec_ref):
    n = RET_BLOCK

    def head(h):
        def run():
            cos2 = cs_ref[0, rows, :]
            sin2 = cs_ref[1, rows, :]

            def rope(t):
                return t * cos2 + pltpu.roll(t, HEAD_DIM // 2, 1) * sin2

            hs = _head_lanes(h)
            q = rope(z_ref[rows, hs])
            k = rope(z_ref[rows, MIX_WIDTH + h * HEAD_DIM:MIX_WIDTH + (h + 1) * HEAD_DIM])
            v = z_ref[rows, 2 * MIX_WIDTH + h * HEAD_DIM:2 * MIX_WIDTH + (h + 1) * HEAD_DIM]
            zg = z_ref[rows, 3 * MIX_WIDTH + h * HEAD_DIM:3 * MIX_WIDTH + (h + 1) * HEAD_DIM]
            qb = q.astype(_BF16)
            vb = v.astype(_BF16)
            scores = _dot_nt(qb, k.astype(_BF16)) * dec_ref[h]
            st = st_ref[h]
            o = _dot(scores.astype(_BF16), vb) + _dot_nt(qb, st.astype(_BF16)) * qdec_ref[h]
            k_s = k * kdec_ref[h]
            st_ref[h] = st * math.exp(_log_gamma(h) * n) + _dot_tn(vb, k_s.astype(_BF16))
            inv = lax.rsqrt(jnp.mean(o * o, axis=-1, keepdims=True) + EPS)
            cat_ref[rows, hs] = o * (inv * onorm_ref[:, hs]) * (zg / (1.0 + jnp.exp(-zg)))
        return run

    return [head(h) for h in range(MIX_HEADS)]


def _rotary_item(pos_ref, crows, freq_ref, cs_ref):
    def run():
        ang = pos_ref[0, crows, :].astype(_F32) * freq_ref[...]
        lane = lax.broadcasted_iota(jnp.int32, ang.shape, 1)
        cs_ref[0] = jnp.cos(ang)
        cs_ref[1] = jnp.where(lane < HEAD_DIM // 2, -1.0, 1.0) * jnp.sin(ang)
    return run


def _in_proj_items(x_ref, crows, nmix_ref, win_ref, hn_ref, z_ref):
    def norm():
        hn_ref[...] = _rmsnorm(x_ref[0, crows, :], nmix_ref[...])

    def cols(p):
        def run():
            cs = slice(p * PROJ_COLS, (p + 1) * PROJ_COLS)
            z_ref[:, cs] = _dot(hn_ref[...], win_ref[:, cs])
        return run

    return [norm] + [cols(p) for p in range(IN_WIDTH // PROJ_COLS)]


def _out_proj_items(x_ref, crows, cat_ref, wout_ref, out_ref):
    def cols(p):
        def run():
            cs = slice(p * PROJ_COLS, (p + 1) * PROJ_COLS)
            out_ref[0, crows, cs] = x_ref[0, crows, cs] + _dot(cat_ref[...], wout_ref[:, cs])
        return run

    return [cols(p) for p in range(D_MODEL // PROJ_COLS)]


def _run_interleaved(main, filler):
    done = 0
    for k, item in enumerate(main):
        item()
        upto = (k + 1) * len(filler) // len(main)
        for f in filler[done:upto]:
            f()
        done = upto
    for f in filler[done:]:
        f()


def _mixer_kernel(*refs, kind):
    if kind == "hgrn":
        (x_ref, mem_ref, nmix_ref, win_ref, wout_ref, nmem_ref, wkv_ref, lb_ref, onorm_ref,
         out_ref, z_ref, hn_ref, cat_ref, st_ref, mkT_ref, mv_ref) = refs
    else:
        (x_ref, mem_ref, nmix_ref, win_ref, wout_ref, nmem_ref, wkv_ref, pos_ref, freq_ref, onorm_ref,
         out_ref, z_ref, hn_ref, cat_ref, st_ref, mkT_ref, mv_ref, cs_ref, dec_ref, qdec_ref, kdec_ref) = refs

    @pl.when(pl.program_id(1) == 0)
    def _start_of_sequence():
        st_ref[...] = jnp.zeros_like(st_ref)
        _memory_kv(mem_ref, nmem_ref, wkv_ref, mkT_ref, mv_ref)
        if kind != "hgrn":
            _retention_tables(dec_ref, qdec_ref, kdec_ref)

    chunk = CHUNK[kind]
    n_chunks = SEQ_TILE // chunk

    def chunk_rows(c):
        return slice(c * chunk, (c + 1) * chunk)

    def stage_in(c):
        return _in_proj_items(x_ref, chunk_rows(c), nmix_ref, win_ref, hn_ref, z_ref.at[c % 2])

    def stage_mix(c):
        zc, cc = z_ref.at[c % 2], cat_ref.at[c % 2]
        whole = slice(0, chunk)
        xattn = _cross_attention_items(zc, whole, mkT_ref, mv_ref, cc)
        if kind == "hgrn":
            items = xattn
            for i in range(chunk // HGRN_BLOCK):
                rows = slice(i * HGRN_BLOCK, (i + 1) * HGRN_BLOCK)
                items += _hgrn_items(zc, rows, lb_ref, onorm_ref, st_ref, cc)
            return items
        items = []
        for i in range(chunk // RET_BLOCK):
            rows = slice(i * RET_BLOCK, (i + 1) * RET_BLOCK)
            items += _retention_items(zc, rows, cs_ref.at[c], onorm_ref, st_ref, cc, dec_ref, qdec_ref, kdec_ref)
        return items + xattn

    def stage_out(c):
        return _out_proj_items(x_ref, chunk_rows(c), cat_ref.at[c % 2], wout_ref, out_ref)

    if kind != "hgrn":
        for c in range(n_chunks):
            _rotary_item(pos_ref, chunk_rows(c), freq_ref, cs_ref.at[c])()
    for item in stage_in(0):
        item()
    for c in range(n_chunks):
        filler = (stage_out(c - 1) if c > 0 else []) + (stage_in(c + 1) if c + 1 < n_chunks else [])
        _run_interleaved(stage_mix(c), filler)
    for item in stage_out(n_chunks - 1):
        item()


def _mixer_layer(kind, layer, x, mem, norm_mix, w_in, w_out, norm_mem, w_mem_kv, extra):
    bsz, seq, _ = x.shape
    const2 = lambda b, t: (0, 0)
    layer3 = lambda b, t: (layer, 0, 0)
    in_specs = [
        pl.BlockSpec((1, SEQ_TILE, D_MODEL), lambda b, t: (b, t, 0)),
        pl.BlockSpec((1, N_MEM, D_MODEL), lambda b, t: (b, 0, 0)),
        pl.BlockSpec((1, D_MODEL), const2),
        pl.BlockSpec((None, D_MODEL, IN_WIDTH), layer3, pipeline_mode=pl.Buffered(1)),
        pl.BlockSpec((None, CAT_WIDTH, D_MODEL), layer3, pipeline_mode=pl.Buffered(1)),
        pl.BlockSpec((1, D_MODEL), const2),
        pl.BlockSpec((None, D_MODEL, 2 * XATTN_WIDTH), layer3, pipeline_mode=pl.Buffered(1)),
    ]
    chunk = CHUNK[kind]
    n_slots = min(2, SEQ_TILE // chunk)
    scratch = [
        pltpu.VMEM((n_slots, chunk, IN_WIDTH), _F32),
        pltpu.VMEM((chunk, D_MODEL), _F32),
        pltpu.VMEM((n_slots, chunk, CAT_WIDTH), _F32),
        pltpu.VMEM((MIX_HEADS, HEAD_DIM, HEAD_DIM), _F32),
        pltpu.VMEM((XATTN_WIDTH, N_MEM), _BF16),
        pltpu.VMEM((N_XATTN_HEADS * N_MEM, XATTN_WIDTH), _BF16),
    ]
    if kind == "hgrn":
        in_specs += [pl.BlockSpec((1, MIX_WIDTH), const2), pl.BlockSpec((1, MIX_WIDTH), const2)]
    else:
        in_specs += [
            pl.BlockSpec((1, SEQ_TILE, 1), lambda b, t: (b, t, 0)),
            pl.BlockSpec((1, HEAD_DIM), const2),
            pl.BlockSpec((1, MIX_WIDTH), const2),
        ]
        scratch += [
            pltpu.VMEM((SEQ_TILE // chunk, 2, chunk, HEAD_DIM), _F32),
            pltpu.VMEM((MIX_HEADS, RET_BLOCK, RET_BLOCK), _F32),
            pltpu.VMEM((MIX_HEADS, RET_BLOCK, HEAD_DIM), _F32),
            pltpu.VMEM((MIX_HEADS, RET_BLOCK, HEAD_DIM), _F32),
        ]
    return pl.pallas_call(
        functools.partial(_mixer_kernel, kind=kind),
        grid=(bsz, seq // SEQ_TILE),
        in_specs=in_specs,
        out_specs=pl.BlockSpec((1, SEQ_TILE, D_MODEL), lambda b, t: (b, t, 0)),
        out_shape=jax.ShapeDtypeStruct(x.shape, x.dtype),
        scratch_shapes=scratch,
        compiler_params=pltpu.CompilerParams(
            dimension_semantics=("arbitrary", "arbitrary"), vmem_limit_bytes=VMEM_LIMIT_BYTES),
        name=f"{kind}_mixer_layer",
    )(x, mem, norm_mix, w_in, w_out, norm_mem, w_mem_kv, *extra)


def _ffn_kernel(*refs, final):
    if final:
        x_ref, n_ref, win_ref, wout_ref, nf_ref, out_ref = refs
    else:
        x_ref, n_ref, win_ref, wout_ref, out_ref = refs
    x = x_ref[...]
    hn = _rmsnorm(x, n_ref[...])
    acc = x
    assert sum(FFN_CHUNKS) == D_FF
    lo = 0
    for width in FFN_CHUNKS:
        hi = lo + width
        g = _dot(hn, win_ref[:, lo:hi])
        u = _dot(hn, win_ref[:, D_FF + lo:D_FF + hi])
        a = g * jax.nn.sigmoid(g) * u
        acc = acc + _dot(a, wout_ref[lo:hi, :])
        lo = hi
    if final:
        acc = _rmsnorm(acc, nf_ref[...])
    out_ref[...] = acc


def _ffn_layer(x2d, layer, norm_ffn, w_ffn_in, w_ffn_out, norm_final):
    final = norm_final is not None
    const = lambda i: (0, 0)
    in_specs = [
        pl.BlockSpec((FFN_TILE, D_MODEL), lambda i: (i, 0)),
        pl.BlockSpec((1, D_MODEL), const),
        pl.BlockSpec((None, D_MODEL, 2 * D_FF), lambda i: (layer, 0, 0), pipeline_mode=pl.Buffered(1)),
        pl.BlockSpec((None, D_FF, D_MODEL), lambda i: (layer, 0, 0), pipeline_mode=pl.Buffered(1)),
    ]
    args = [x2d, norm_ffn, w_ffn_in, w_ffn_out]
    if final:
        in_specs.append(pl.BlockSpec((1, D_MODEL), const))
        args.append(norm_final)
    return pl.pallas_call(
        functools.partial(_ffn_kernel, final=final),
        grid=(x2d.shape[0] // FFN_TILE,),
        in_specs=in_specs,
        out_specs=pl.BlockSpec((FFN_TILE, D_MODEL), lambda i: (i, 0)),
        out_shape=jax.ShapeDtypeStruct(x2d.shape, x2d.dtype),
        compiler_params=pltpu.CompilerParams(
            dimension_semantics=("arbitrary",), vmem_limit_bytes=VMEM_LIMIT_BYTES),
        name="swiglu_ffn_final" if final else "swiglu_ffn",
    )(*args)


def kernel(x, mem, positions, norm_mix, w_in, w_out, norm_mem, w_mem_kv, hgrn_lb_logits, hgrn_out_norm,
           ret_out_norm, norm_ffn, w_ffn_in, w_ffn_out, norm_final):
    bsz, seq, _ = x.shape
    depth = w_in.shape[0]
    assert seq % SEQ_TILE == 0 and (bsz * seq) % FFN_TILE == 0

    p_lb = jax.nn.softmax(hgrn_lb_logits.astype(_F32), axis=0)
    lower_bounds = jnp.cumsum(p_lb, axis=0) - p_lb[0]
    half = HEAD_DIM // 2
    inv_freq = ROPE_BASE ** (-jnp.linspace(0.0, 1.0, half, dtype=_F32))
    inv_freq2 = jnp.concatenate([inv_freq, inv_freq]).reshape(1, HEAD_DIM)
    pos3 = positions.reshape(bsz, seq, 1)

    for i in range(depth):
        j = i // N_MIXERS
        common = (i, x, mem, norm_mix[i].reshape(1, D_MODEL), w_in, w_out, norm_mem[i].reshape(1, D_MODEL), w_mem_kv)
        if i % N_MIXERS == 0:
            extra = (lower_bounds[j].reshape(1, MIX_WIDTH), hgrn_out_norm[j].reshape(1, MIX_WIDTH))
            x = _mixer_layer("hgrn", *common, extra)
        else:
            extra = (pos3, inv_freq2, ret_out_norm[j].reshape(1, MIX_WIDTH))
            x = _mixer_layer("retention", *common, extra)
        x = _ffn_layer(
            x.reshape(bsz * seq, D_MODEL), i, norm_ffn[i].reshape(1, D_MODEL), w_ffn_in, w_ffn_out,
            norm_final.reshape(1, D_MODEL) if i == depth - 1 else None,
        ).reshape(bsz, seq, D_MODEL)
    return x
```

```python
import functools
import math

import jax
import jax.numpy as jnp
from jax import lax
from jax.experimental import pallas as pl
from jax.experimental.pallas import tpu as pltpu

D_MODEL = 1024
MIX_HEADS = 6
HEAD_DIM = 128
MIX_WIDTH = MIX_HEADS * HEAD_DIM
N_XATTN_HEADS = 4
XATTN_HEAD_DIM = 64
XATTN_WIDTH = N_XATTN_HEADS * XATTN_HEAD_DIM
N_MEM = 256
IN_WIDTH = 4 * MIX_WIDTH + XATTN_WIDTH
CAT_WIDTH = MIX_WIDTH + XATTN_WIDTH
D_FF = 2816
N_MIXERS = 2
ROPE_BASE = 10000.0
EPS = 1e-6
EXP_CLAMP = 30.0

SEQ_TILE = {"hgrn": 1024, "retention": 1024}
CHUNK_SLOTS = {"hgrn": 2, "retention": 1}
CHUNK = {"hgrn": 256, "retention": 512}
PROJ_COLS = 256
HGRN_BLOCK = 128
HGRN_SUB = 32
N_SUB = HGRN_BLOCK // HGRN_SUB
RET_BLOCK = 256
FFN_TILE = 512
MXU_WIDTH = 256
FFN_CHUNKS = (6 * MXU_WIDTH, 5 * MXU_WIDTH)
VMEM_LIMIT_BYTES = 60 * 1024 * 1024

_BF16 = jnp.bfloat16
_F32 = jnp.float32


def _dot(a, b):
    return jnp.dot(a, b, preferred_element_type=_F32)


def _dot_nt(a, b):
    return lax.dot_general(a, b, (((1,), (1,)), ((), ())), preferred_element_type=_F32)


def _dot_tn(a, b):
    return lax.dot_general(a, b, (((0,), (0,)), ((), ())), preferred_element_type=_F32)


def _group_index(idx, group):
    shift = group.bit_length() - 1
    assert 1 << shift == group
    return lax.shift_right_logical(idx, jnp.int32(shift))


def _rmsnorm(x, w):
    return x * lax.rsqrt(jnp.mean(x * x, axis=-1, keepdims=True) + EPS) * w


def _head_lanes(h):
    return slice(h * HEAD_DIM, (h + 1) * HEAD_DIM)


def _log_gamma(h):
    return math.log(1.0 - 2.0 ** (-5.0 - h))


def _memory_kv(mem_ref, nmem_ref, wkv_ref, mkT_ref, mv_ref):
    memh = _rmsnorm(mem_ref[0], nmem_ref[...])
    mkv = _dot(memh, wkv_ref[...])
    mkT_ref[...] = (mkv[:, :XATTN_WIDTH] * (XATTN_HEAD_DIM ** -0.5)).T.astype(_BF16)
    mv = mkv[:, XATTN_WIDTH:]
    lane_head = _group_index(lax.broadcasted_iota(jnp.int32, (N_MEM, XATTN_WIDTH), 1), XATTN_HEAD_DIM)
    for h in range(N_XATTN_HEADS):
        mv_ref[h * N_MEM:(h + 1) * N_MEM, :] = jnp.where(lane_head == h, mv, 0.0).astype(_BF16)


def _cross_attention_items(z_ref, rows, mkT_ref, mv_ref, cat_ref):
    ctx = {}

    def scores():
        q = z_ref[rows, 4 * MIX_WIDTH:].astype(_BF16)
        lane_head = _group_index(lax.broadcasted_iota(jnp.int32, q.shape, 1), XATTN_HEAD_DIM)
        q_heads = [jnp.where(lane_head == h, q, jnp.zeros_like(q)) for h in range(N_XATTN_HEADS)]
        ctx["s"] = _dot(jnp.concatenate(q_heads, axis=0), mkT_ref[...])

    def softmax(h):
        def run():
            r = ctx["s"].shape[0] // N_XATTN_HEADS
            s = ctx["s"][h * r:(h + 1) * r]
            e = jnp.exp(s - jnp.max(s, axis=-1, keepdims=True))
            ctx[h] = (e / jnp.sum(e, axis=-1, keepdims=True)).astype(_BF16)
        return run

    def outputs():
        p = jnp.concatenate([ctx.pop(h) for h in range(N_XATTN_HEADS)], axis=1)
        cat_ref[rows, MIX_WIDTH:] = _dot(p, mv_ref[...])

    return [scores] + [softmax(h) for h in range(N_XATTN_HEADS)] + [outputs]


def _sub_rows(a, c):
    return a[c * HGRN_SUB:(c + 1) * HGRN_SUB]


def _scale_subs(a, vecs):
    parts = []
    for c, vec in enumerate(vecs):
        part = jnp.zeros((HGRN_SUB, a.shape[1]), _F32) if vec is None else _sub_rows(a, c) * vec
        parts.append(part.astype(_BF16))
    return jnp.concatenate(parts, axis=0)


def _hgrn_items(z_ref, rows, lb_ref, onorm_ref, st_ref, cat_ref):
    n = HGRN_BLOCK
    exp_clamp = math.exp(EXP_CLAMP)
    ops = {}
    outs = {}

    def masks():
        ti = lax.broadcasted_iota(jnp.int32, (n, n), 0)
        si = lax.broadcasted_iota(jnp.int32, (n, n), 1)
        mask_a = (_group_index(ti, HGRN_SUB) == _group_index(si, HGRN_SUB)) & (si <= ti)
        mask_b = _group_index(ti, 2 * HGRN_SUB) == _group_index(si, 2 * HGRN_SUB)
        return mask_a, mask_b, si <= ti

    def gates(h):
        def run():
            hs = _head_lanes(h)
            za = z_ref[rows, hs]
            fr = z_ref[rows, MIX_WIDTH + h * HEAD_DIM:MIX_WIDTH + (h + 1) * HEAD_DIM]
            lb = lb_ref[:, hs]
            q = za / (1.0 + jnp.exp(-za))
            e = jnp.exp(-jnp.abs(fr))
            r = 1.0 / (1.0 + e)
            nonneg = fr >= 0.0
            k = (1.0 - lb) * (jnp.where(nonneg, e, 1.0) * r)
            e_neg = jnp.where(nonneg, e, jnp.minimum(1.0 / e, exp_clamp))
            g = jnp.minimum(fr, 0.0) + jnp.log((1.0 + lb * e_neg) * r)
            g_hi = g.astype(_BF16)
            g_lo = (g - g_hi.astype(_F32)).astype(_BF16)
            lower_ones = jnp.where(masks()[2], 1.0, 0.0).astype(_BF16)
            bb = _dot(lower_ones, jnp.concatenate([g_hi, g_lo], axis=1))
            b = bb[:, :HEAD_DIM] + bb[:, HEAD_DIM:]

            mids = [b[c * HGRN_SUB + HGRN_SUB // 2 - 1:c * HGRN_SUB + HGRN_SUB // 2, :] for c in range(N_SUB)]
            ends = [b[(c + 1) * HGRN_SUB - 1:(c + 1) * HGRN_SUB, :] for c in range(N_SUB)]
            ref = jnp.concatenate([jnp.broadcast_to(m, (HGRN_SUB, HEAD_DIM)) for m in mids], axis=0)
            q_a = q * jnp.exp(b - ref)
            k_a = k * jnp.exp(ref - b)
            ops[h] = dict(
                q_a=q_a.astype(_BF16), k_a=k_a.astype(_BF16),
                q_b=_scale_subs(q_a, [None, jnp.exp(mids[1] - ends[0]), None, jnp.exp(mids[3] - ends[2])]),
                k_b=_scale_subs(k_a, [jnp.exp(ends[0] - mids[0]), None, jnp.exp(ends[2] - mids[2]), None]),
                q_c=_scale_subs(q_a, [None, None, jnp.exp(mids[2] - ends[1]), jnp.exp(mids[3] - ends[1])]),
                k_c=_scale_subs(k_a, [jnp.exp(ends[1] - mids[0]), jnp.exp(ends[1] - mids[1]), None, None]),
                q_o=_scale_subs(q_a, [jnp.exp(m) for m in mids]),
                k_s=_scale_subs(k_a, [jnp.exp(ends[N_SUB - 1] - m) for m in mids]),
                decay=jnp.exp(ends[N_SUB - 1]),
            )
        return run

    def mix(h):
        def run():
            op = ops.pop(h)
            mask_a, mask_b, _ = masks()
            vb = z_ref[rows, 2 * MIX_WIDTH + h * HEAD_DIM:2 * MIX_WIDTH + (h + 1) * HEAD_DIM].astype(_BF16)
            s_a = _dot_nt(op["q_a"], op["k_a"])
            s_b = _dot_nt(op["q_b"], op["k_b"])
            s_c = _dot_nt(op["q_c"], op["k_c"])
            scores = jnp.where(mask_a, s_a, 0.0) + jnp.where(mask_b, s_b, 0.0) + s_c
            st = st_ref[h]
            outs[h] = _dot(scores.astype(_BF16), vb) + _dot_nt(op["q_o"], st.astype(_BF16))
            st_ref[h] = st * op["decay"] + _dot_tn(vb, op["k_s"])
        return run

    def finish():
        ss = None
        for h in range(MIX_HEADS):
            part = jnp.sum(outs[h] * outs[h], axis=-1, keepdims=True)
            ss = part if ss is None else ss + part
        inv = lax.rsqrt(ss * (1.0 / MIX_WIDTH) + EPS)
        for h in range(MIX_HEADS):
            hs = _head_lanes(h)
            zg = z_ref[rows, 3 * MIX_WIDTH + h * HEAD_DIM:3 * MIX_WIDTH + (h + 1) * HEAD_DIM]
            cat_ref[rows, hs] = outs[h] * (inv * onorm_ref[:, hs]) / (1.0 + jnp.exp(-zg))

    heads = range(MIX_HEADS)
    return [gates(h) for h in heads] + [mix(h) for h in heads] + [finish]


def _retention_tables(dec_ref, qdec_ref, kdec_ref):
    n = RET_BLOCK
    scale = HEAD_DIM ** -0.5
    ti = lax.broadcasted_iota(jnp.int32, (n, n), 0)
    si = lax.broadcasted_iota(jnp.int32, (n, n), 1)
    rel = (ti - si).astype(_F32)
    pos = lax.broadcasted_iota(jnp.int32, (n, HEAD_DIM), 0).astype(_F32)
    for h in range(MIX_HEADS):
        lg = math.log(1.0 - 2.0 ** (-5.0 - h))
        dec_ref[h] = jnp.where(ti >= si, jnp.exp(lg * jnp.maximum(rel, 0.0)) * scale, 0.0)
        qdec_ref[h] = jnp.exp(lg * (pos + 1.0))
        kdec_ref[h] = jnp.exp(lg * (n - 1.0 - pos)) * scale


def _retention_items(z_ref, rows, cs_ref, onorm_ref, st_ref, cat_ref, dec_ref, qdec_ref, kdec_ref):
    n = RET_BLOCK

    def head(h):
        def run():
            cos2 = cs_ref[0, rows, :]
            sin2 = cs_ref[1, rows, :]

            def rope(t):
                return t * cos2 + pltpu.roll(t, HEAD_DIM // 2, 1) * sin2

            hs = _head_lanes(h)
            q = rope(z_ref[rows, hs])
            k = rope(z_ref[rows, MIX_WIDTH + h * HEAD_DIM:MIX_WIDTH + (h + 1) * HEAD_DIM])
            v = z_ref[rows, 2 * MIX_WIDTH + h * HEAD_DIM:2 * MIX_WIDTH + (h + 1) * HEAD_DIM]
            zg = z_ref[rows, 3 * MIX_WIDTH + h * HEAD_DIM:3 * MIX_WIDTH + (h + 1) * HEAD_DIM]
            qb = q.astype(_BF16)
            vb = v.astype(_BF16)
            scores = _dot_nt(qb, k.astype(_BF16)) * dec_ref[h]
            st = st_ref[h]
            o = _dot(scores.astype(_BF16), vb) + _dot_nt(qb, st.astype(_BF16)) * qdec_ref[h]
            k_s = k * kdec_ref[h]
            st_ref[h] = st * math.exp(_log_gamma(h) * n) + _dot_tn(vb, k_s.astype(_BF16))
            inv = lax.rsqrt(jnp.mean(o * o, axis=-1, keepdims=True) + EPS)
            cat_ref[rows, hs] = o * (inv * onorm_ref[:, hs]) * (zg / (1.0 + jnp.exp(-zg)))
        return run

    return [head(h) for h in range(MIX_HEADS)]


def _rotary_item(pos_ref, crows, freq_ref, cs_ref):
    def run():
        ang = pos_ref[0, crows, :].astype(_F32) * freq_ref[...]
        lane = lax.broadcasted_iota(jnp.int32, ang.shape, 1)
        cs_ref[0] = jnp.cos(ang)
        cs_ref[1] = jnp.where(lane < HEAD_DIM // 2, -1.0, 1.0) * jnp.sin(ang)
    return run


def _in_proj_items(x_ref, crows, nmix_ref, win_ref, hn_ref, z_ref):
    def norm():
        hn_ref[...] = _rmsnorm(x_ref[0, crows, :], nmix_ref[...])

    def cols(p):
        def run():
            cs = slice(p * PROJ_COLS, (p + 1) * PROJ_COLS)
            z_ref[:, cs] = _dot(hn_ref[...], win_ref[:, cs])
        return run

    return [norm] + [cols(p) for p in range(IN_WIDTH // PROJ_COLS)]


def _out_proj_items(x_ref, crows, cat_ref, wout_ref, out_ref):
    def cols(p):
        def run():
            cs = slice(p * PROJ_COLS, (p + 1) * PROJ_COLS)
            out_ref[0, crows, cs] = x_ref[0, crows, cs] + _dot(cat_ref[...], wout_ref[:, cs])
        return run

    return [cols(p) for p in range(D_MODEL // PROJ_COLS)]


def _run_interleaved(main, filler):
    done = 0
    for k, item in enumerate(main):
        item()
        upto = (k + 1) * len(filler) // len(main)
        for f in filler[done:upto]:
            f()
        done = upto
    for f in filler[done:]:
        f()


def _mixer_kernel(*refs, kind):
    if kind == "hgrn":
        (x_ref, mem_ref, nmix_ref, win_ref, wout_ref, nmem_ref, wkv_ref, lb_ref, onorm_ref,
         out_ref, z_ref, hn_ref, cat_ref, st_ref, mkT_ref, mv_ref) = refs
    else:
        (x_ref, mem_ref, nmix_ref, win_ref, wout_ref, nmem_ref, wkv_ref, pos_ref, freq_ref, onorm_ref,
         out_ref, z_ref, hn_ref, cat_ref, st_ref, mkT_ref, mv_ref, cs_ref, dec_ref, qdec_ref, kdec_ref) = refs

    @pl.when(pl.program_id(1) == 0)
    def _start_of_sequence():
        st_ref[...] = jnp.zeros_like(st_ref)
        _memory_kv(mem_ref, nmem_ref, wkv_ref, mkT_ref, mv_ref)
        if kind != "hgrn":
            _retention_tables(dec_ref, qdec_ref, kdec_ref)

    chunk = CHUNK[kind]
    n_chunks = SEQ_TILE[kind] // chunk
    n_slots = z_ref.shape[0]

    def chunk_rows(c):
        return slice(c * chunk, (c + 1) * chunk)

    def stage_in(c):
        return _in_proj_items(x_ref, chunk_rows(c), nmix_ref, win_ref, hn_ref, z_ref.at[c % n_slots])

    def stage_mix(c):
        zc, cc = z_ref.at[c % n_slots], cat_ref.at[c % n_slots]
        whole = slice(0, chunk)
        xattn = _cross_attention_items(zc, whole, mkT_ref, mv_ref, cc)
        if kind == "hgrn":
            items = xattn
            for i in range(chunk // HGRN_BLOCK):
                rows = slice(i * HGRN_BLOCK, (i + 1) * HGRN_BLOCK)
                items += _hgrn_items(zc, rows, lb_ref, onorm_ref, st_ref, cc)
            return items
        items = []
        for i in range(chunk // RET_BLOCK):
            rows = slice(i * RET_BLOCK, (i + 1) * RET_BLOCK)
            items += _retention_items(zc, rows, cs_ref.at[c], onorm_ref, st_ref, cc, dec_ref, qdec_ref, kdec_ref)
        return items + xattn

    def stage_out(c):
        return _out_proj_items(x_ref, chunk_rows(c), cat_ref.at[c % n_slots], wout_ref, out_ref)

    if kind != "hgrn":
        for c in range(n_chunks):
            _rotary_item(pos_ref, chunk_rows(c), freq_ref, cs_ref.at[c])()
    if n_slots == 1:
        for c in range(n_chunks):
            for item in stage_in(c) + stage_mix(c) + stage_out(c):
                item()
        return
    for item in stage_in(0):
        item()
    for c in range(n_chunks):
        filler = (stage_out(c - 1) if c > 0 else []) + (stage_in(c + 1) if c + 1 < n_chunks else [])
        _run_interleaved(stage_mix(c), filler)
    for item in stage_out(n_chunks - 1):
        item()


def _mixer_layer(kind, layer, x, mem, norm_mix, w_in, w_out, norm_mem, w_mem_kv, extra):
    bsz, seq, _ = x.shape
    seq_tile = SEQ_TILE[kind]
    const2 = lambda b, t: (0, 0)
    layer3 = lambda b, t: (layer, 0, 0)
    in_specs = [
        pl.BlockSpec((1, seq_tile, D_MODEL), lambda b, t: (b, t, 0)),
        pl.BlockSpec((1, N_MEM, D_MODEL), lambda b, t: (b, 0, 0)),
        pl.BlockSpec((1, D_MODEL), const2),
        pl.BlockSpec((None, D_MODEL, IN_WIDTH), layer3, pipeline_mode=pl.Buffered(1)),
        pl.BlockSpec((None, CAT_WIDTH, D_MODEL), layer3, pipeline_mode=pl.Buffered(1)),
        pl.BlockSpec((1, D_MODEL), const2),
        pl.BlockSpec((None, D_MODEL, 2 * XATTN_WIDTH), layer3, pipeline_mode=pl.Buffered(1)),
    ]
    chunk = CHUNK[kind]
    n_slots = min(CHUNK_SLOTS[kind], seq_tile // chunk)
    scratch = [
        pltpu.VMEM((n_slots, chunk, IN_WIDTH), _F32),
        pltpu.VMEM((chunk, D_MODEL), _F32),
        pltpu.VMEM((n_slots, chunk, CAT_WIDTH), _F32),
        pltpu.VMEM((MIX_HEADS, HEAD_DIM, HEAD_DIM), _F32),
        pltpu.VMEM((XATTN_WIDTH, N_MEM), _BF16),
        pltpu.VMEM((N_XATTN_HEADS * N_MEM, XATTN_WIDTH), _BF16),
    ]
    if kind == "hgrn":
        in_specs += [pl.BlockSpec((1, MIX_WIDTH), const2), pl.BlockSpec((1, MIX_WIDTH), const2)]
    else:
        in_specs += [
            pl.BlockSpec((1, seq_tile, 1), lambda b, t: (b, t, 0)),
            pl.BlockSpec((1, HEAD_DIM), const2),
            pl.BlockSpec((1, MIX_WIDTH), const2),
        ]
        scratch += [
            pltpu.VMEM((seq_tile // chunk, 2, chunk, HEAD_DIM), _F32),
            pltpu.VMEM((MIX_HEADS, RET_BLOCK, RET_BLOCK), _F32),
            pltpu.VMEM((MIX_HEADS, RET_BLOCK, HEAD_DIM), _F32),
            pltpu.VMEM((MIX_HEADS, RET_BLOCK, HEAD_DIM), _F32),
        ]
    return pl.pallas_call(
        functools.partial(_mixer_kernel, kind=kind),
        grid=(bsz, seq // seq_tile),
        in_specs=in_specs,
        out_specs=pl.BlockSpec((1, seq_tile, D_MODEL), lambda b, t: (b, t, 0)),
        out_shape=jax.ShapeDtypeStruct(x.shape, x.dtype),
        scratch_shapes=scratch,
        compiler_params=pltpu.CompilerParams(
            dimension_semantics=("arbitrary", "arbitrary"), vmem_limit_bytes=VMEM_LIMIT_BYTES),
        name=f"{kind}_mixer_layer",
    )(x, mem, norm_mix, w_in, w_out, norm_mem, w_mem_kv, *extra)


def _ffn_kernel(*refs, final):
    if final:
        x_ref, n_ref, win_ref, wout_ref, nf_ref, out_ref = refs
    else:
        x_ref, n_ref, win_ref, wout_ref, out_ref = refs
    x = x_ref[...]
    hn = _rmsnorm(x, n_ref[...])
    acc = x
    assert sum(FFN_CHUNKS) == D_FF
    lo = 0
    for width in FFN_CHUNKS:
        hi = lo + width
        g = _dot(hn, win_ref[:, lo:hi])
        u = _dot(hn, win_ref[:, D_FF + lo:D_FF + hi])
        a = g * jax.nn.sigmoid(g) * u
        acc = acc + _dot(a, wout_ref[lo:hi, :])
        lo = hi
    if final:
        acc = _rmsnorm(acc, nf_ref[...])
    out_ref[...] = acc


def _ffn_layer(x2d, layer, norm_ffn, w_ffn_in, w_ffn_out, norm_final):
    final = norm_final is not None
    const = lambda i: (0, 0)
    in_specs = [
        pl.BlockSpec((FFN_TILE, D_MODEL), lambda i: (i, 0)),
        pl.BlockSpec((1, D_MODEL), const),
        pl.BlockSpec((None, D_MODEL, 2 * D_FF), lambda i: (layer, 0, 0), pipeline_mode=pl.Buffered(1)),
        pl.BlockSpec((None, D_FF, D_MODEL), lambda i: (layer, 0, 0), pipeline_mode=pl.Buffered(1)),
    ]
    args = [x2d, norm_ffn, w_ffn_in, w_ffn_out]
    if final:
        in_specs.append(pl.BlockSpec((1, D_MODEL), const))
        args.append(norm_final)
    return pl.pallas_call(
        functools.partial(_ffn_kernel, final=final),
        grid=(x2d.shape[0] // FFN_TILE,),
        in_specs=in_specs,
        out_specs=pl.BlockSpec((FFN_TILE, D_MODEL), lambda i: (i, 0)),
        out_shape=jax.ShapeDtypeStruct(x2d.shape, x2d.dtype),
        compiler_params=pltpu.CompilerParams(
            dimension_semantics=("arbitrary",), vmem_limit_bytes=VMEM_LIMIT_BYTES),
        name="swiglu_ffn_final" if final else "swiglu_ffn",
    )(*args)


def kernel(x, mem, positions, norm_mix, w_in, w_out, norm_mem, w_mem_kv, hgrn_lb_logits, hgrn_out_norm,
           ret_out_norm, norm_ffn, w_ffn_in, w_ffn_out, norm_final):
    bsz, seq, _ = x.shape
    depth = w_in.shape[0]
    assert all(seq % tile == 0 for tile in SEQ_TILE.values()) and (bsz * seq) % FFN_TILE == 0

    p_lb = jax.nn.softmax(hgrn_lb_logits.astype(_F32), axis=0)
    lower_bounds = jnp.cumsum(p_lb, axis=0) - p_lb[0]
    half = HEAD_DIM // 2
    inv_freq = ROPE_BASE ** (-jnp.linspace(0.0, 1.0, half, dtype=_F32))
    inv_freq2 = jnp.concatenate([inv_freq, inv_freq]).reshape(1, HEAD_DIM)
    pos3 = positions.reshape(bsz, seq, 1)

    for i in range(depth):
        j = i // N_MIXERS
        common = (i, x, mem, norm_mix[i].reshape(1, D_MODEL), w_in, w_out, norm_mem[i].reshape(1, D_MODEL), w_mem_kv)
        if i % N_MIXERS == 0:
            extra = (lower_bounds[j].reshape(1, MIX_WIDTH), hgrn_out_norm[j].reshape(1, MIX_WIDTH))
            x = _mixer_layer("hgrn", *common, extra)
        else:
            extra = (pos3, inv_freq2, ret_out_norm[j].reshape(1, MIX_WIDTH))
            x = _mixer_layer("retention", *common, extra)
        x = _ffn_layer(
            x.reshape(bsz * seq, D_MODEL), i, norm_ffn[i].reshape(1, D_MODEL), w_ffn_in, w_ffn_out,
            norm_final.reshape(1, D_MODEL) if i == depth - 1 else None,
        ).reshape(bsz, seq, D_MODEL)
    return x
```

```python
import functools
import math

import jax
import jax.numpy as jnp
from jax import lax
from jax.experimental import pallas as pl
from jax.experimental.pallas import tpu as pltpu

D_MODEL = 1024
MIX_HEADS = 6
HEAD_DIM = 128
MIX_WIDTH = MIX_HEADS * HEAD_DIM
N_XATTN_HEADS = 4
XATTN_HEAD_DIM = 64
XATTN_WIDTH = N_XATTN_HEADS * XATTN_HEAD_DIM
N_MEM = 256
IN_WIDTH = 4 * MIX_WIDTH + XATTN_WIDTH
CAT_WIDTH = MIX_WIDTH + XATTN_WIDTH
D_FF = 2816
N_MIXERS = 2
ROPE_BASE = 10000.0
EPS = 1e-6
EXP_CLAMP = 30.0

SEQ_TILE = {"hgrn": 1024, "retention": 1024}
CHUNK_SLOTS = {"hgrn": 2, "retention": 1}
CHUNK = {"hgrn": 256, "retention": 512}
PROJ_COLS = 256
HGRN_BLOCK = 128
HGRN_SUB = 32
N_SUB = HGRN_BLOCK // HGRN_SUB
RET_BLOCK = 256
FFN_TILE = 512
MXU_WIDTH = 256
FFN_CHUNKS = (6 * MXU_WIDTH, 5 * MXU_WIDTH)
VMEM_LIMIT_BYTES = 60 * 1024 * 1024

_BF16 = jnp.bfloat16
_F32 = jnp.float32


def _dot(a, b):
    return jnp.dot(a, b, preferred_element_type=_F32)


def _dot_nt(a, b):
    return lax.dot_general(a, b, (((1,), (1,)), ((), ())), preferred_element_type=_F32)


def _dot_tn(a, b):
    return lax.dot_general(a, b, (((0,), (0,)), ((), ())), preferred_element_type=_F32)


def _group_index(idx, group):
    shift = group.bit_length() - 1
    assert 1 << shift == group
    return lax.shift_right_logical(idx, jnp.int32(shift))


def _rmsnorm(x, w):
    return x * lax.rsqrt(jnp.mean(x * x, axis=-1, keepdims=True) + EPS) * w


def _head_lanes(h):
    return slice(h * HEAD_DIM, (h + 1) * HEAD_DIM)


def _log_gamma(h):
    return math.log(1.0 - 2.0 ** (-5.0 - h))


def _memory_kv(mem_ref, nmem_ref, wkv_ref, mkT_ref, mv_ref):
    memh = _rmsnorm(mem_ref[0], nmem_ref[...])
    mkv = _dot(memh, wkv_ref[...])
    mkT_ref[...] = (mkv[:, :XATTN_WIDTH] * (XATTN_HEAD_DIM ** -0.5)).T.astype(_BF16)
    mv = mkv[:, XATTN_WIDTH:]
    lane_head = _group_index(lax.broadcasted_iota(jnp.int32, (N_MEM, XATTN_WIDTH), 1), XATTN_HEAD_DIM)
    for h in range(N_XATTN_HEADS):
        mv_ref[h * N_MEM:(h + 1) * N_MEM, :] = jnp.where(lane_head == h, mv, 0.0).astype(_BF16)


def _cross_attention_items(z_ref, rows, mkT_ref, mv_ref, cat_ref):
    ctx = {}

    def scores():
        q = z_ref[rows, 4 * MIX_WIDTH:].astype(_BF16)
        lane_head = _group_index(lax.broadcasted_iota(jnp.int32, q.shape, 1), XATTN_HEAD_DIM)
        q_heads = [jnp.where(lane_head == h, q, jnp.zeros_like(q)) for h in range(N_XATTN_HEADS)]
        ctx["s"] = _dot(jnp.concatenate(q_heads, axis=0), mkT_ref[...])

    def softmax(h):
        def run():
            r = ctx["s"].shape[0] // N_XATTN_HEADS
            s = ctx["s"][h * r:(h + 1) * r]
            e = jnp.exp(s - jnp.max(s, axis=-1, keepdims=True))
            ctx[h] = (e / jnp.sum(e, axis=-1, keepdims=True)).astype(_BF16)
        return run

    def outputs():
        p = jnp.concatenate([ctx.pop(h) for h in range(N_XATTN_HEADS)], axis=1)
        cat_ref[rows, MIX_WIDTH:] = _dot(p, mv_ref[...])

    return [scores] + [softmax(h) for h in range(N_XATTN_HEADS)] + [outputs]


def _sub_rows(a, c):
    return a[c * HGRN_SUB:(c + 1) * HGRN_SUB]


def _scale_subs(a, vecs):
    parts = []
    for c, vec in enumerate(vecs):
        part = jnp.zeros((HGRN_SUB, a.shape[1]), _F32) if vec is None else _sub_rows(a, c) * vec
        parts.append(part.astype(_BF16))
    return jnp.concatenate(parts, axis=0)


def _hgrn_items(z_ref, rows, lb_ref, onorm_ref, st_ref, cat_ref):
    n = HGRN_BLOCK
    exp_clamp = math.exp(EXP_CLAMP)
    ops = {}
    outs = {}

    def masks():
        ti = lax.broadcasted_iota(jnp.int32, (n, n), 0)
        si = lax.broadcasted_iota(jnp.int32, (n, n), 1)
        mask_a = (_group_index(ti, HGRN_SUB) == _group_index(si, HGRN_SUB)) & (si <= ti)
        mask_b = _group_index(ti, 2 * HGRN_SUB) == _group_index(si, 2 * HGRN_SUB)
        return mask_a, mask_b, si <= ti

    def gates(h):
        def run():
            hs = _head_lanes(h)
            za = z_ref[rows, hs]
            fr = z_ref[rows, MIX_WIDTH + h * HEAD_DIM:MIX_WIDTH + (h + 1) * HEAD_DIM]
            lb = lb_ref[:, hs]
            q = za / (1.0 + jnp.exp(-za))
            e = jnp.exp(-jnp.abs(fr))
            r = 1.0 / (1.0 + e)
            nonneg = fr >= 0.0
            k = (1.0 - lb) * (jnp.where(nonneg, e, 1.0) * r)
            e_neg = jnp.where(nonneg, e, jnp.minimum(1.0 / e, exp_clamp))
            g = jnp.minimum(fr, 0.0) + jnp.log((1.0 + lb * e_neg) * r)
            g_hi = g.astype(_BF16)
            g_lo = (g - g_hi.astype(_F32)).astype(_BF16)
            lower_ones = jnp.where(masks()[2], 1.0, 0.0).astype(_BF16)
            bb = _dot(lower_ones, jnp.concatenate([g_hi, g_lo], axis=1))
            b = bb[:, :HEAD_DIM] + bb[:, HEAD_DIM:]

            mids = [b[c * HGRN_SUB + HGRN_SUB // 2 - 1:c * HGRN_SUB + HGRN_SUB // 2, :] for c in range(N_SUB)]
            ends = [b[(c + 1) * HGRN_SUB - 1:(c + 1) * HGRN_SUB, :] for c in range(N_SUB)]
            ref = jnp.concatenate([jnp.broadcast_to(m, (HGRN_SUB, HEAD_DIM)) for m in mids], axis=0)
            q_a = q * jnp.exp(b - ref)
            k_a = k * jnp.exp(ref - b)
            ops[h] = dict(
                q_a=q_a.astype(_BF16), k_a=k_a.astype(_BF16),
                q_b=_scale_subs(q_a, [None, jnp.exp(mids[1] - ends[0]), None, jnp.exp(mids[3] - ends[2])]),
                k_b=_scale_subs(k_a, [jnp.exp(ends[0] - mids[0]), None, jnp.exp(ends[2] - mids[2]), None]),
                q_c=_scale_subs(q_a, [None, None, jnp.exp(mids[2] - ends[1]), jnp.exp(mids[3] - ends[1])]),
                k_c=_scale_subs(k_a, [jnp.exp(ends[1] - mids[0]), jnp.exp(ends[1] - mids[1]), None, None]),
                q_o=_scale_subs(q_a, [jnp.exp(m) for m in mids]),
                k_s=_scale_subs(k_a, [jnp.exp(ends[N_SUB - 1] - m) for m in mids]),
                decay=jnp.exp(ends[N_SUB - 1]),
            )
        return run

    def mix(h):
        def run():
            op = ops.pop(h)
            mask_a, mask_b, _ = masks()
            vb = z_ref[rows, 2 * MIX_WIDTH + h * HEAD_DIM:2 * MIX_WIDTH + (h + 1) * HEAD_DIM].astype(_BF16)
            s_a = _dot_nt(op["q_a"], op["k_a"])
            s_b = _dot_nt(op["q_b"], op["k_b"])
            s_c = _dot_nt(op["q_c"], op["k_c"])
            scores = jnp.where(mask_a, s_a, 0.0) + jnp.where(mask_b, s_b, 0.0) + s_c
            st = st_ref[h]
            outs[h] = _dot(scores.astype(_BF16), vb) + _dot_nt(op["q_o"], st.astype(_BF16))
            st_ref[h] = st * op["decay"] + _dot_tn(vb, op["k_s"])
        return run

    def finish():
        ss = None
        for h in range(MIX_HEADS):
            part = jnp.sum(outs[h] * outs[h], axis=-1, keepdims=True)
            ss = part if ss is None else ss + part
        inv = lax.rsqrt(ss * (1.0 / MIX_WIDTH) + EPS)
        for h in range(MIX_HEADS):
            hs = _head_lanes(h)
            zg = z_ref[rows, 3 * MIX_WIDTH + h * HEAD_DIM:3 * MIX_WIDTH + (h + 1) * HEAD_DIM]
            cat_ref[rows, hs] = outs[h] * (inv * onorm_ref[:, hs]) / (1.0 + jnp.exp(-zg))

    heads = range(MIX_HEADS)
    return [gates(h) for h in heads] + [mix(h) for h in heads] + [finish]


def _retention_tables(dec_ref, qdec_ref, kdec_ref):
    n = RET_BLOCK
    scale = HEAD_DIM ** -0.5
    ti = lax.broadcasted_iota(jnp.int32, (n, n), 0)
    si = lax.broadcasted_iota(jnp.int32, (n, n), 1)
    rel = (ti - si).astype(_F32)
    pos = lax.broadcasted_iota(jnp.int32, (n, HEAD_DIM), 0).astype(_F32)
    for h in range(MIX_HEADS):
        lg = math.log(1.0 - 2.0 ** (-5.0 - h))
        dec_ref[h] = jnp.where(ti >= si, jnp.exp(lg * jnp.maximum(rel, 0.0)) * scale, 0.0)
        qdec_ref[h] = jnp.exp(lg * (pos + 1.0))
        kdec_ref[h] = jnp.exp(lg * (n - 1.0 - pos)) * scale


def _retention_items(z_ref, rows, cs_ref, onorm_ref, st_ref, cat_ref, dec_ref, qdec_ref, kdec_ref):
    n = RET_BLOCK

    def head(h):
        def run():
            cos2 = cs_ref[0, rows, :]
            sin2 = cs_ref[1, rows, :]

            def rope(t):
                return t * cos2 + pltpu.roll(t, HEAD_DIM // 2, 1) * sin2

            hs = _head_lanes(h)
            q = rope(z_ref[rows, hs])
            k = rope(z_ref[rows, MIX_WIDTH + h * HEAD_DIM:MIX_WIDTH + (h + 1) * HEAD_DIM])
            v = z_ref[rows, 2 * MIX_WIDTH + h * HEAD_DIM:2 * MIX_WIDTH + (h + 1) * HEAD_DIM]
            zg = z_ref[rows, 3 * MIX_WIDTH + h * HEAD_DIM:3 * MIX_WIDTH + (h + 1) * HEAD_DIM]
            qb = q.astype(_BF16)
            vb = v.astype(_BF16)
            scores = _dot_nt(qb, k.astype(_BF16)) * dec_ref[h]
            st = st_ref[h]
            o = _dot(scores.astype(_BF16), vb) + _dot_nt(qb, st.astype(_BF16)) * qdec_ref[h]
            k_s = k * kdec_ref[h]
            st_ref[h] = st * math.exp(_log_gamma(h) * n) + _dot_tn(vb, k_s.astype(_BF16))
            inv = lax.rsqrt(jnp.mean(o * o, axis=-1, keepdims=True) + EPS)
            cat_ref[rows, hs] = o * (inv * onorm_ref[:, hs]) * (zg / (1.0 + jnp.exp(-zg)))
        return run

    return [head(h) for h in range(MIX_HEADS)]


def _rotary_item(pos_ref, crows, freq_ref, cs_ref):
    def run():
        half_rows = (crows.stop - crows.start) // 2
        pos = pos_ref[0, crows, :].astype(_F32)
        lane = lax.broadcasted_iota(jnp.int32, (half_rows, HEAD_DIM), 1)
        low = lane < HEAD_DIM // 2
        ang = jnp.where(low, pos[:half_rows], pos[half_rows:]) * freq_ref[...]
        cos, sin = jnp.cos(ang), jnp.sin(ang)
        cos_sw, sin_sw = pltpu.roll(cos, HEAD_DIM // 2, 1), pltpu.roll(sin, HEAD_DIM // 2, 1)
        cs_ref[0, :half_rows, :] = jnp.where(low, cos, cos_sw)
        cs_ref[0, half_rows:, :] = jnp.where(low, cos_sw, cos)
        cs_ref[1, :half_rows, :] = jnp.where(low, -sin, sin_sw)
        cs_ref[1, half_rows:, :] = jnp.where(low, -sin_sw, sin)
    return run


def _in_proj_items(x_ref, crows, nmix_ref, win_ref, hn_ref, z_ref):
    def norm():
        hn_ref[...] = _rmsnorm(x_ref[0, crows, :], nmix_ref[...])

    def cols(p):
        def run():
            cs = slice(p * PROJ_COLS, (p + 1) * PROJ_COLS)
            z_ref[:, cs] = _dot(hn_ref[...], win_ref[:, cs])
        return run

    return [norm] + [cols(p) for p in range(IN_WIDTH // PROJ_COLS)]


def _out_proj_items(x_ref, crows, cat_ref, wout_ref, out_ref):
    def cols(p):
        def run():
            cs = slice(p * PROJ_COLS, (p + 1) * PROJ_COLS)
            out_ref[0, crows, cs] = x_ref[0, crows, cs] + _dot(cat_ref[...], wout_ref[:, cs])
        return run

    return [cols(p) for p in range(D_MODEL // PROJ_COLS)]


def _run_interleaved(main, filler):
    done = 0
    for k, item in enumerate(main):
        item()
        upto = (k + 1) * len(filler) // len(main)
        for f in filler[done:upto]:
            f()
        done = upto
    for f in filler[done:]:
        f()


def _mixer_kernel(*refs, kind):
    if kind == "hgrn":
        (x_ref, mem_ref, nmix_ref, win_ref, wout_ref, nmem_ref, wkv_ref, lb_ref, onorm_ref,
         out_ref, z_ref, hn_ref, cat_ref, st_ref, mkT_ref, mv_ref) = refs
    else:
        (x_ref, mem_ref, nmix_ref, win_ref, wout_ref, nmem_ref, wkv_ref, pos_ref, freq_ref, onorm_ref,
         out_ref, z_ref, hn_ref, cat_ref, st_ref, mkT_ref, mv_ref, cs_ref, dec_ref, qdec_ref, kdec_ref) = refs

    @pl.when(pl.program_id(1) == 0)
    def _start_of_sequence():
        st_ref[...] = jnp.zeros_like(st_ref)
        _memory_kv(mem_ref, nmem_ref, wkv_ref, mkT_ref, mv_ref)
        if kind != "hgrn":
            _retention_tables(dec_ref, qdec_ref, kdec_ref)

    chunk = CHUNK[kind]
    n_chunks = SEQ_TILE[kind] // chunk
    n_slots = z_ref.shape[0]

    def chunk_rows(c):
        return slice(c * chunk, (c + 1) * chunk)

    def stage_in(c):
        return _in_proj_items(x_ref, chunk_rows(c), nmix_ref, win_ref, hn_ref, z_ref.at[c % n_slots])

    def stage_mix(c):
        zc, cc = z_ref.at[c % n_slots], cat_ref.at[c % n_slots]
        whole = slice(0, chunk)
        xattn = _cross_attention_items(zc, whole, mkT_ref, mv_ref, cc)
        if kind == "hgrn":
            items = xattn
            for i in range(chunk // HGRN_BLOCK):
                rows = slice(i * HGRN_BLOCK, (i + 1) * HGRN_BLOCK)
                items += _hgrn_items(zc, rows, lb_ref, onorm_ref, st_ref, cc)
            return items
        items = []
        for i in range(chunk // RET_BLOCK):
            rows = slice(i * RET_BLOCK, (i + 1) * RET_BLOCK)
            items += _retention_items(zc, rows, cs_ref.at[c], onorm_ref, st_ref, cc, dec_ref, qdec_ref, kdec_ref)
        return items + xattn

    def stage_out(c):
        return _out_proj_items(x_ref, chunk_rows(c), cat_ref.at[c % n_slots], wout_ref, out_ref)

    if kind != "hgrn":
        for c in range(n_chunks):
            _rotary_item(pos_ref, chunk_rows(c), freq_ref, cs_ref.at[c])()
    if n_slots == 1:
        for c in range(n_chunks):
            for item in stage_in(c) + stage_mix(c) + stage_out(c):
                item()
        return
    for item in stage_in(0):
        item()
    for c in range(n_chunks):
        filler = (stage_out(c - 1) if c > 0 else []) + (stage_in(c + 1) if c + 1 < n_chunks else [])
        _run_interleaved(stage_mix(c), filler)
    for item in stage_out(n_chunks - 1):
        item()


def _mixer_layer(kind, layer, x, mem, norm_mix, w_in, w_out, norm_mem, w_mem_kv, extra):
    bsz, seq, _ = x.shape
    seq_tile = SEQ_TILE[kind]
    const2 = lambda b, t: (0, 0)
    layer3 = lambda b, t: (layer, 0, 0)
    in_specs = [
        pl.BlockSpec((1, seq_tile, D_MODEL), lambda b, t: (b, t, 0)),
        pl.BlockSpec((1, N_MEM, D_MODEL), lambda b, t: (b, 0, 0)),
        pl.BlockSpec((1, D_MODEL), const2),
        pl.BlockSpec((None, D_MODEL, IN_WIDTH), layer3, pipeline_mode=pl.Buffered(1)),
        pl.BlockSpec((None, CAT_WIDTH, D_MODEL), layer3, pipeline_mode=pl.Buffered(1)),
        pl.BlockSpec((1, D_MODEL), const2),
        pl.BlockSpec((None, D_MODEL, 2 * XATTN_WIDTH), layer3, pipeline_mode=pl.Buffered(1)),
    ]
    chunk = CHUNK[kind]
    n_slots = min(CHUNK_SLOTS[kind], seq_tile // chunk)
    scratch = [
        pltpu.VMEM((n_slots, chunk, IN_WIDTH), _F32),
        pltpu.VMEM((chunk, D_MODEL), _F32),
        pltpu.VMEM((n_slots, chunk, CAT_WIDTH), _F32),
        pltpu.VMEM((MIX_HEADS, HEAD_DIM, HEAD_DIM), _F32),
        pltpu.VMEM((XATTN_WIDTH, N_MEM), _BF16),
        pltpu.VMEM((N_XATTN_HEADS * N_MEM, XATTN_WIDTH), _BF16),
    ]
    if kind == "hgrn":
        in_specs += [pl.BlockSpec((1, MIX_WIDTH), const2), pl.BlockSpec((1, MIX_WIDTH), const2)]
    else:
        in_specs += [
            pl.BlockSpec((1, seq_tile, 1), lambda b, t: (b, t, 0)),
            pl.BlockSpec((1, HEAD_DIM), const2),
            pl.BlockSpec((1, MIX_WIDTH), const2),
        ]
        scratch += [
            pltpu.VMEM((seq_tile // chunk, 2, chunk, HEAD_DIM), _F32),
            pltpu.VMEM((MIX_HEADS, RET_BLOCK, RET_BLOCK), _F32),
            pltpu.VMEM((MIX_HEADS, RET_BLOCK, HEAD_DIM), _F32),
            pltpu.VMEM((MIX_HEADS, RET_BLOCK, HEAD_DIM), _F32),
        ]
    return pl.pallas_call(
        functools.partial(_mixer_kernel, kind=kind),
        grid=(bsz, seq // seq_tile),
        in_specs=in_specs,
        out_specs=pl.BlockSpec((1, seq_tile, D_MODEL), lambda b, t: (b, t, 0)),
        out_shape=jax.ShapeDtypeStruct(x.shape, x.dtype),
        scratch_shapes=scratch,
        compiler_params=pltpu.CompilerParams(
            dimension_semantics=("arbitrary", "arbitrary"), vmem_limit_bytes=VMEM_LIMIT_BYTES),
        name=f"{kind}_mixer_layer",
    )(x, mem, norm_mix, w_in, w_out, norm_mem, w_mem_kv, *extra)


def _ffn_kernel(*refs, final):
    if final:
        x_ref, n_ref, win_ref, wout_ref, nf_ref, out_ref = refs
    else:
        x_ref, n_ref, win_ref, wout_ref, out_ref = refs
    x = x_ref[...]
    hn = _rmsnorm(x, n_ref[...])
    acc = x
    assert sum(FFN_CHUNKS) == D_FF
    lo = 0
    for width in FFN_CHUNKS:
        hi = lo + width
        g = _dot(hn, win_ref[:, lo:hi])
        u = _dot(hn, win_ref[:, D_FF + lo:D_FF + hi])
        a = g * jax.nn.sigmoid(g) * u
        acc = acc + _dot(a, wout_ref[lo:hi, :])
        lo = hi
    if final:
        acc = _rmsnorm(acc, nf_ref[...])
    out_ref[...] = acc


def _ffn_layer(x2d, layer, norm_ffn, w_ffn_in, w_ffn_out, norm_final):
    final = norm_final is not None
    const = lambda i: (0, 0)
    in_specs = [
        pl.BlockSpec((FFN_TILE, D_MODEL), lambda i: (i, 0)),
        pl.BlockSpec((1, D_MODEL), const),
        pl.BlockSpec((None, D_MODEL, 2 * D_FF), lambda i: (layer, 0, 0), pipeline_mode=pl.Buffered(1)),
        pl.BlockSpec((None, D_FF, D_MODEL), lambda i: (layer, 0, 0), pipeline_mode=pl.Buffered(1)),
    ]
    args = [x2d, norm_ffn, w_ffn_in, w_ffn_out]
    if final:
        in_specs.append(pl.BlockSpec((1, D_MODEL), const))
        args.append(norm_final)
    return pl.pallas_call(
        functools.partial(_ffn_kernel, final=final),
        grid=(x2d.shape[0] // FFN_TILE,),
        in_specs=in_specs,
        out_specs=pl.BlockSpec((FFN_TILE, D_MODEL), lambda i: (i, 0)),
        out_shape=jax.ShapeDtypeStruct(x2d.shape, x2d.dtype),
        compiler_params=pltpu.CompilerParams(
            dimension_semantics=("arbitrary",), vmem_limit_bytes=VMEM_LIMIT_BYTES),
        name="swiglu_ffn_final" if final else "swiglu_ffn",
    )(*args)


def kernel(x, mem, positions, norm_mix, w_in, w_out, norm_mem, w_mem_kv, hgrn_lb_logits, hgrn_out_norm,
           ret_out_norm, norm_ffn, w_ffn_in, w_ffn_out, norm_final):
    bsz, seq, _ = x.shape
    depth = w_in.shape[0]
    assert all(seq % tile == 0 for tile in SEQ_TILE.values()) and (bsz * seq) % FFN_TILE == 0

    p_lb = jax.nn.softmax(hgrn_lb_logits.astype(_F32), axis=0)
    lower_bounds = jnp.cumsum(p_lb, axis=0) - p_lb[0]
    half = HEAD_DIM // 2
    inv_freq = ROPE_BASE ** (-jnp.linspace(0.0, 1.0, half, dtype=_F32))
    inv_freq2 = jnp.concatenate([inv_freq, inv_freq]).reshape(1, HEAD_DIM)
    pos3 = positions.reshape(bsz, seq, 1)

    for i in range(depth):
        j = i // N_MIXERS
        common = (i, x, mem, norm_mix[i].reshape(1, D_MODEL), w_in, w_out, norm_mem[i].reshape(1, D_MODEL), w_mem_kv)
        if i % N_MIXERS == 0:
            extra = (lower_bounds[j].reshape(1, MIX_WIDTH), hgrn_out_norm[j].reshape(1, MIX_WIDTH))
            x = _mixer_layer("hgrn", *common, extra)
        else:
            extra = (pos3, inv_freq2, ret_out_norm[j].reshape(1, MIX_WIDTH))
            x = _mixer_layer("retention", *common, extra)
        x = _ffn_layer(
            x.reshape(bsz * seq, D_MODEL), i, norm_ffn[i].reshape(1, D_MODEL), w_ffn_in, w_ffn_out,
            norm_final.reshape(1, D_MODEL) if i == depth - 1 else None,
        ).reshape(bsz, seq, D_MODEL)
    return x
```

```python
import functools
import math

import jax
import jax.numpy as jnp
from jax import lax
from jax.experimental import pallas as pl
from jax.experimental.pallas import tpu as pltpu

D_MODEL = 1024
MIX_HEADS = 6
HEAD_DIM = 128
MIX_WIDTH = MIX_HEADS * HEAD_DIM
N_XATTN_HEADS = 4
XATTN_HEAD_DIM = 64
XATTN_WIDTH = N_XATTN_HEADS * XATTN_HEAD_DIM
N_MEM = 256
IN_WIDTH = 4 * MIX_WIDTH + XATTN_WIDTH
CAT_WIDTH = MIX_WIDTH + XATTN_WIDTH
D_FF = 2816
N_MIXERS = 2
ROPE_BASE = 10000.0
EPS = 1e-6
EXP_CLAMP = 30.0

SEQ_TILE = {"hgrn": 1024, "retention": 1024}
CHUNK_SLOTS = {"hgrn": 2, "retention": 1}
CHUNK = {"hgrn": 256, "retention": 512}
PROJ_COLS = 256
HGRN_BLOCK = 128
HGRN_SUB = 32
N_SUB = HGRN_BLOCK // HGRN_SUB
RET_BLOCK = 256
FFN_TILE = 512
MXU_WIDTH = 256
FFN_CHUNKS = (MXU_WIDTH,) * 11
VMEM_LIMIT_BYTES = 60 * 1024 * 1024

_BF16 = jnp.bfloat16
_F32 = jnp.float32


def _dot(a, b):
    return jnp.dot(a, b, preferred_element_type=_F32)


def _dot_nt(a, b):
    return lax.dot_general(a, b, (((1,), (1,)), ((), ())), preferred_element_type=_F32)


def _dot_tn(a, b):
    return lax.dot_general(a, b, (((0,), (0,)), ((), ())), preferred_element_type=_F32)


def _group_index(idx, group):
    shift = group.bit_length() - 1
    assert 1 << shift == group
    return lax.shift_right_logical(idx, jnp.int32(shift))


def _rmsnorm(x, w):
    return x * lax.rsqrt(jnp.mean(x * x, axis=-1, keepdims=True) + EPS) * w


def _head_lanes(h):
    return slice(h * HEAD_DIM, (h + 1) * HEAD_DIM)


def _log_gamma(h):
    return math.log(1.0 - 2.0 ** (-5.0 - h))


def _memory_kv(mem_ref, nmem_ref, wkv_ref, mkT_ref, mv_ref):
    memh = _rmsnorm(mem_ref[0], nmem_ref[...])
    mkv = _dot(memh, wkv_ref[...])
    mkT_ref[...] = (mkv[:, :XATTN_WIDTH] * (XATTN_HEAD_DIM ** -0.5)).T.astype(_BF16)
    mv = mkv[:, XATTN_WIDTH:]
    lane_head = _group_index(lax.broadcasted_iota(jnp.int32, (N_MEM, XATTN_WIDTH), 1), XATTN_HEAD_DIM)
    for h in range(N_XATTN_HEADS):
        mv_ref[h * N_MEM:(h + 1) * N_MEM, :] = jnp.where(lane_head == h, mv, 0.0).astype(_BF16)


def _cross_attention_items(z_ref, rows, mkT_ref, mv_ref, cat_ref):
    ctx = {}

    def scores():
        q = z_ref[rows, 4 * MIX_WIDTH:].astype(_BF16)
        lane_head = _group_index(lax.broadcasted_iota(jnp.int32, q.shape, 1), XATTN_HEAD_DIM)
        q_heads = [jnp.where(lane_head == h, q, jnp.zeros_like(q)) for h in range(N_XATTN_HEADS)]
        ctx["s"] = _dot(jnp.concatenate(q_heads, axis=0), mkT_ref[...])

    def softmax(h):
        def run():
            r = ctx["s"].shape[0] // N_XATTN_HEADS
            s = ctx["s"][h * r:(h + 1) * r]
            e = jnp.exp(s - jnp.max(s, axis=-1, keepdims=True))
            ctx[h] = (e / jnp.sum(e, axis=-1, keepdims=True)).astype(_BF16)
        return run

    def outputs():
        p = jnp.concatenate([ctx.pop(h) for h in range(N_XATTN_HEADS)], axis=1)
        cat_ref[rows, MIX_WIDTH:] = _dot(p, mv_ref[...])

    return [scores] + [softmax(h) for h in range(N_XATTN_HEADS)] + [outputs]


def _sub_rows(a, c):
    return a[c * HGRN_SUB:(c + 1) * HGRN_SUB]


def _scale_subs(a, vecs):
    parts = []
    for c, vec in enumerate(vecs):
        part = jnp.zeros((HGRN_SUB, a.shape[1]), _F32) if vec is None else _sub_rows(a, c) * vec
        parts.append(part.astype(_BF16))
    return jnp.concatenate(parts, axis=0)


def _hgrn_items(z_ref, rows, lb_ref, onorm_ref, st_ref, cat_ref):
    n = HGRN_BLOCK
    exp_clamp = math.exp(EXP_CLAMP)
    ops = {}
    outs = {}

    def masks():
        ti = lax.broadcasted_iota(jnp.int32, (n, n), 0)
        si = lax.broadcasted_iota(jnp.int32, (n, n), 1)
        mask_a = (_group_index(ti, HGRN_SUB) == _group_index(si, HGRN_SUB)) & (si <= ti)
        mask_b = _group_index(ti, 2 * HGRN_SUB) == _group_index(si, 2 * HGRN_SUB)
        return mask_a, mask_b, si <= ti

    def gates(h):
        def run():
            hs = _head_lanes(h)
            za = z_ref[rows, hs]
            fr = z_ref[rows, MIX_WIDTH + h * HEAD_DIM:MIX_WIDTH + (h + 1) * HEAD_DIM]
            lb = lb_ref[:, hs]
            q = za / (1.0 + jnp.exp(-za))
            e = jnp.exp(-jnp.abs(fr))
            r = 1.0 / (1.0 + e)
            nonneg = fr >= 0.0
            k = (1.0 - lb) * (jnp.where(nonneg, e, 1.0) * r)
            e_neg = jnp.where(nonneg, e, jnp.minimum(1.0 / e, exp_clamp))
            g = jnp.minimum(fr, 0.0) + jnp.log((1.0 + lb * e_neg) * r)
            g_hi = g.astype(_BF16)
            g_lo = (g - g_hi.astype(_F32)).astype(_BF16)
            lower_ones = jnp.where(masks()[2], 1.0, 0.0).astype(_BF16)
            bb = _dot(lower_ones, jnp.concatenate([g_hi, g_lo], axis=1))
            b = bb[:, :HEAD_DIM] + bb[:, HEAD_DIM:]

            mids = [b[c * HGRN_SUB + HGRN_SUB // 2 - 1:c * HGRN_SUB + HGRN_SUB // 2, :] for c in range(N_SUB)]
            ends = [b[(c + 1) * HGRN_SUB - 1:(c + 1) * HGRN_SUB, :] for c in range(N_SUB)]
            ref = jnp.concatenate([jnp.broadcast_to(m, (HGRN_SUB, HEAD_DIM)) for m in mids], axis=0)
            q_a = q * jnp.exp(b - ref)
            k_a = k * jnp.exp(ref - b)
            ops[h] = dict(
                q_a=q_a.astype(_BF16), k_a=k_a.astype(_BF16),
                q_b=_scale_subs(q_a, [None, jnp.exp(mids[1] - ends[0]), None, jnp.exp(mids[3] - ends[2])]),
                k_b=_scale_subs(k_a, [jnp.exp(ends[0] - mids[0]), None, jnp.exp(ends[2] - mids[2]), None]),
                q_c=_scale_subs(q_a, [None, None, jnp.exp(mids[2] - ends[1]), jnp.exp(mids[3] - ends[1])]),
                k_c=_scale_subs(k_a, [jnp.exp(ends[1] - mids[0]), jnp.exp(ends[1] - mids[1]), None, None]),
                q_o=_scale_subs(q_a, [jnp.exp(m) for m in mids]),
                k_s=_scale_subs(k_a, [jnp.exp(ends[N_SUB - 1] - m) for m in mids]),
                decay=jnp.exp(ends[N_SUB - 1]),
            )
        return run

    def mix(h):
        def run():
            op = ops.pop(h)
            mask_a, mask_b, _ = masks()
            vb = z_ref[rows, 2 * MIX_WIDTH + h * HEAD_DIM:2 * MIX_WIDTH + (h + 1) * HEAD_DIM].astype(_BF16)
            s_a = _dot_nt(op["q_a"], op["k_a"])
            s_b = _dot_nt(op["q_b"], op["k_b"])
            s_c = _dot_nt(op["q_c"], op["k_c"])
            scores = jnp.where(mask_a, s_a, 0.0) + jnp.where(mask_b, s_b, 0.0) + s_c
            st = st_ref[h]
            outs[h] = _dot(scores.astype(_BF16), vb) + _dot_nt(op["q_o"], st.astype(_BF16))
            st_ref[h] = st * op["decay"] + _dot_tn(vb, op["k_s"])
        return run

    def finish():
        ss = None
        for h in range(MIX_HEADS):
            part = jnp.sum(outs[h] * outs[h], axis=-1, keepdims=True)
            ss = part if ss is None else ss + part
        inv = lax.rsqrt(ss * (1.0 / MIX_WIDTH) + EPS)
        for h in range(MIX_HEADS):
            hs = _head_lanes(h)
            zg = z_ref[rows, 3 * MIX_WIDTH + h * HEAD_DIM:3 * MIX_WIDTH + (h + 1) * HEAD_DIM]
            cat_ref[rows, hs] = outs[h] * (inv * onorm_ref[:, hs]) / (1.0 + jnp.exp(-zg))

    heads = range(MIX_HEADS)
    return [gates(h) for h in heads] + [mix(h) for h in heads] + [finish]


def _retention_tables(dec_ref, qdec_ref, kdec_ref):
    n = RET_BLOCK
    scale = HEAD_DIM ** -0.5
    ti = lax.broadcasted_iota(jnp.int32, (n, n), 0)
    si = lax.broadcasted_iota(jnp.int32, (n, n), 1)
    rel = (ti - si).astype(_F32)
    pos = lax.broadcasted_iota(jnp.int32, (n, HEAD_DIM), 0).astype(_F32)
    for h in range(MIX_HEADS):
        lg = math.log(1.0 - 2.0 ** (-5.0 - h))
        dec_ref[h] = jnp.where(ti >= si, jnp.exp(lg * jnp.maximum(rel, 0.0)) * scale, 0.0)
        qdec_ref[h] = jnp.exp(lg * (pos + 1.0))
        kdec_ref[h] = jnp.exp(lg * (n - 1.0 - pos)) * scale


def _retention_items(z_ref, rows, cs_ref, onorm_ref, st_ref, cat_ref, dec_ref, qdec_ref, kdec_ref):
    n = RET_BLOCK

    def head(h):
        def run():
            cos2 = cs_ref[0, rows, :]
            sin2 = cs_ref[1, rows, :]

            def rope(t):
                return t * cos2 + pltpu.roll(t, HEAD_DIM // 2, 1) * sin2

            hs = _head_lanes(h)
            q = rope(z_ref[rows, hs])
            k = rope(z_ref[rows, MIX_WIDTH + h * HEAD_DIM:MIX_WIDTH + (h + 1) * HEAD_DIM])
            v = z_ref[rows, 2 * MIX_WIDTH + h * HEAD_DIM:2 * MIX_WIDTH + (h + 1) * HEAD_DIM]
            zg = z_ref[rows, 3 * MIX_WIDTH + h * HEAD_DIM:3 * MIX_WIDTH + (h + 1) * HEAD_DIM]
            qb = q.astype(_BF16)
            vb = v.astype(_BF16)
            scores = _dot_nt(qb, k.astype(_BF16)) * dec_ref[h]
            st = st_ref[h]
            o = _dot(scores.astype(_BF16), vb) + _dot_nt(qb, st.astype(_BF16)) * qdec_ref[h]
            k_s = k * kdec_ref[h]
            st_ref[h] = st * math.exp(_log_gamma(h) * n) + _dot_tn(vb, k_s.astype(_BF16))
            inv = lax.rsqrt(jnp.mean(o * o, axis=-1, keepdims=True) + EPS)
            cat_ref[rows, hs] = o * (inv * onorm_ref[:, hs]) * (zg / (1.0 + jnp.exp(-zg)))
        return run

    return [head(h) for h in range(MIX_HEADS)]


def _rotary_item(pos_ref, crows, freq_ref, cs_ref):
    def run():
        half_rows = (crows.stop - crows.start) // 2
        pos = pos_ref[0, crows, :].astype(_F32)
        lane = lax.broadcasted_iota(jnp.int32, (half_rows, HEAD_DIM), 1)
        low = lane < HEAD_DIM // 2
        ang = jnp.where(low, pos[:half_rows], pos[half_rows:]) * freq_ref[...]
        cos, sin = jnp.cos(ang), jnp.sin(ang)
        cos_sw, sin_sw = pltpu.roll(cos, HEAD_DIM // 2, 1), pltpu.roll(sin, HEAD_DIM // 2, 1)
        cs_ref[0, :half_rows, :] = jnp.where(low, cos, cos_sw)
        cs_ref[0, half_rows:, :] = jnp.where(low, cos_sw, cos)
        cs_ref[1, :half_rows, :] = jnp.where(low, -sin, sin_sw)
        cs_ref[1, half_rows:, :] = jnp.where(low, -sin_sw, sin)
    return run


def _in_proj_items(x_ref, crows, nmix_ref, win_ref, hn_ref, z_ref):
    def norm():
        hn_ref[...] = _rmsnorm(x_ref[0, crows, :], nmix_ref[...])

    def cols(p):
        def run():
            cs = slice(p * PROJ_COLS, (p + 1) * PROJ_COLS)
            z_ref[:, cs] = _dot(hn_ref[...], win_ref[:, cs])
        return run

    return [norm] + [cols(p) for p in range(IN_WIDTH // PROJ_COLS)]


def _out_proj_items(x_ref, crows, cat_ref, wout_ref, out_ref):
    def cols(p):
        def run():
            cs = slice(p * PROJ_COLS, (p + 1) * PROJ_COLS)
            out_ref[0, crows, cs] = x_ref[0, crows, cs] + _dot(cat_ref[...], wout_ref[:, cs])
        return run

    return [cols(p) for p in range(D_MODEL // PROJ_COLS)]


def _run_interleaved(main, filler):
    done = 0
    for k, item in enumerate(main):
        item()
        upto = (k + 1) * len(filler) // len(main)
        for f in filler[done:upto]:
            f()
        done = upto
    for f in filler[done:]:
        f()


def _mixer_kernel(*refs, kind):
    if kind == "hgrn":
        (x_ref, mem_ref, nmix_ref, win_ref, wout_ref, nmem_ref, wkv_ref, lb_ref, onorm_ref,
         out_ref, z_ref, hn_ref, cat_ref, st_ref, mkT_ref, mv_ref) = refs
    else:
        (x_ref, mem_ref, nmix_ref, win_ref, wout_ref, nmem_ref, wkv_ref, pos_ref, freq_ref, onorm_ref,
         out_ref, z_ref, hn_ref, cat_ref, st_ref, mkT_ref, mv_ref, cs_ref, dec_ref, qdec_ref, kdec_ref) = refs

    @pl.when(pl.program_id(1) == 0)
    def _start_of_sequence():
        st_ref[...] = jnp.zeros_like(st_ref)
        _memory_kv(mem_ref, nmem_ref, wkv_ref, mkT_ref, mv_ref)
        if kind != "hgrn":
            _retention_tables(dec_ref, qdec_ref, kdec_ref)

    chunk = CHUNK[kind]
    n_chunks = SEQ_TILE[kind] // chunk
    n_slots = z_ref.shape[0]

    def chunk_rows(c):
        return slice(c * chunk, (c + 1) * chunk)

    def stage_in(c):
        return _in_proj_items(x_ref, chunk_rows(c), nmix_ref, win_ref, hn_ref, z_ref.at[c % n_slots])

    def stage_mix(c):
        zc, cc = z_ref.at[c % n_slots], cat_ref.at[c % n_slots]
        whole = slice(0, chunk)
        xattn = _cross_attention_items(zc, whole, mkT_ref, mv_ref, cc)
        if kind == "hgrn":
            items = xattn
            for i in range(chunk // HGRN_BLOCK):
                rows = slice(i * HGRN_BLOCK, (i + 1) * HGRN_BLOCK)
                items += _hgrn_items(zc, rows, lb_ref, onorm_ref, st_ref, cc)
            return items
        items = []
        for i in range(chunk // RET_BLOCK):
            rows = slice(i * RET_BLOCK, (i + 1) * RET_BLOCK)
            items += _retention_items(zc, rows, cs_ref.at[c], onorm_ref, st_ref, cc, dec_ref, qdec_ref, kdec_ref)
        return items + xattn

    def stage_out(c):
        return _out_proj_items(x_ref, chunk_rows(c), cat_ref.at[c % n_slots], wout_ref, out_ref)

    if kind != "hgrn":
        for c in range(n_chunks):
            _rotary_item(pos_ref, chunk_rows(c), freq_ref, cs_ref.at[c])()
    if n_slots == 1:
        for c in range(n_chunks):
            for item in stage_in(c) + stage_mix(c) + stage_out(c):
                item()
        return
    for item in stage_in(0):
        item()
    for c in range(n_chunks):
        filler = (stage_out(c - 1) if c > 0 else []) + (stage_in(c + 1) if c + 1 < n_chunks else [])
        _run_interleaved(stage_mix(c), filler)
    for item in stage_out(n_chunks - 1):
        item()


def _mixer_layer(kind, layer, x, mem, norm_mix, w_in, w_out, norm_mem, w_mem_kv, extra):
    bsz, seq, _ = x.shape
    seq_tile = SEQ_TILE[kind]
    const2 = lambda b, t: (0, 0)
    layer3 = lambda b, t: (layer, 0, 0)
    in_specs = [
        pl.BlockSpec((1, seq_tile, D_MODEL), lambda b, t: (b, t, 0)),
        pl.BlockSpec((1, N_MEM, D_MODEL), lambda b, t: (b, 0, 0)),
        pl.BlockSpec((1, D_MODEL), const2),
        pl.BlockSpec((None, D_MODEL, IN_WIDTH), layer3, pipeline_mode=pl.Buffered(1)),
        pl.BlockSpec((None, CAT_WIDTH, D_MODEL), layer3, pipeline_mode=pl.Buffered(1)),
        pl.BlockSpec((1, D_MODEL), const2),
        pl.BlockSpec((None, D_MODEL, 2 * XATTN_WIDTH), layer3, pipeline_mode=pl.Buffered(1)),
    ]
    chunk = CHUNK[kind]
    n_slots = min(CHUNK_SLOTS[kind], seq_tile // chunk)
    scratch = [
        pltpu.VMEM((n_slots, chunk, IN_WIDTH), _F32),
        pltpu.VMEM((chunk, D_MODEL), _F32),
        pltpu.VMEM((n_slots, chunk, CAT_WIDTH), _F32),
        pltpu.VMEM((MIX_HEADS, HEAD_DIM, HEAD_DIM), _F32),
        pltpu.VMEM((XATTN_WIDTH, N_MEM), _BF16),
        pltpu.VMEM((N_XATTN_HEADS * N_MEM, XATTN_WIDTH), _BF16),
    ]
    if kind == "hgrn":
        in_specs += [pl.BlockSpec((1, MIX_WIDTH), const2), pl.BlockSpec((1, MIX_WIDTH), const2)]
    else:
        in_specs += [
            pl.BlockSpec((1, seq_tile, 1), lambda b, t: (b, t, 0)),
            pl.BlockSpec((1, HEAD_DIM), const2),
            pl.BlockSpec((1, MIX_WIDTH), const2),
        ]
        scratch += [
            pltpu.VMEM((seq_tile // chunk, 2, chunk, HEAD_DIM), _F32),
            pltpu.VMEM((MIX_HEADS, RET_BLOCK, RET_BLOCK), _F32),
            pltpu.VMEM((MIX_HEADS, RET_BLOCK, HEAD_DIM), _F32),
            pltpu.VMEM((MIX_HEADS, RET_BLOCK, HEAD_DIM), _F32),
        ]
    return pl.pallas_call(
        functools.partial(_mixer_kernel, kind=kind),
        grid=(bsz, seq // seq_tile),
        in_specs=in_specs,
        out_specs=pl.BlockSpec((1, seq_tile, D_MODEL), lambda b, t: (b, t, 0)),
        out_shape=jax.ShapeDtypeStruct(x.shape, x.dtype),
        scratch_shapes=scratch,
        compiler_params=pltpu.CompilerParams(
            dimension_semantics=("arbitrary", "arbitrary"), vmem_limit_bytes=VMEM_LIMIT_BYTES),
        name=f"{kind}_mixer_layer",
    )(x, mem, norm_mix, w_in, w_out, norm_mem, w_mem_kv, *extra)


def _ffn_kernel(*refs, final):
    if final:
        x_ref, n_ref, win_ref, wout_ref, nf_ref, out_ref = refs
    else:
        x_ref, n_ref, win_ref, wout_ref, out_ref = refs
    x = x_ref[...]
    hn = _rmsnorm(x, n_ref[...])
    acc = x
    assert sum(FFN_CHUNKS) == D_FF
    lo = 0
    for width in FFN_CHUNKS:
        hi = lo + width
        g = _dot(hn, win_ref[:, lo:hi])
        u = _dot(hn, win_ref[:, D_FF + lo:D_FF + hi])
        a = g * jax.nn.sigmoid(g) * u
        acc = acc + _dot(a, wout_ref[lo:hi, :])
        lo = hi
    if final:
        acc = _rmsnorm(acc, nf_ref[...])
    out_ref[...] = acc


def _ffn_layer(x2d, layer, norm_ffn, w_ffn_in, w_ffn_out, norm_final):
    final = norm_final is not None
    const = lambda i: (0, 0)
    in_specs = [
        pl.BlockSpec((FFN_TILE, D_MODEL), lambda i: (i, 0)),
        pl.BlockSpec((1, D_MODEL), const),
        pl.BlockSpec((None, D_MODEL, 2 * D_FF), lambda i: (layer, 0, 0), pipeline_mode=pl.Buffered(1)),
        pl.BlockSpec((None, D_FF, D_MODEL), lambda i: (layer, 0, 0), pipeline_mode=pl.Buffered(1)),
    ]
    args = [x2d, norm_ffn, w_ffn_in, w_ffn_out]
    if final:
        in_specs.append(pl.BlockSpec((1, D_MODEL), const))
        args.append(norm_final)
    return pl.pallas_call(
        functools.partial(_ffn_kernel, final=final),
        grid=(x2d.shape[0] // FFN_TILE,),
        in_specs=in_specs,
        out_specs=pl.BlockSpec((FFN_TILE, D_MODEL), lambda i: (i, 0)),
        out_shape=jax.ShapeDtypeStruct(x2d.shape, x2d.dtype),
        compiler_params=pltpu.CompilerParams(
            dimension_semantics=("arbitrary",), vmem_limit_bytes=VMEM_LIMIT_BYTES),
        name="swiglu_ffn_final" if final else "swiglu_ffn",
    )(*args)


def kernel(x, mem, positions, norm_mix, w_in, w_out, norm_mem, w_mem_kv, hgrn_lb_logits, hgrn_out_norm,
           ret_out_norm, norm_ffn, w_ffn_in, w_ffn_out, norm_final):
    bsz, seq, _ = x.shape
    depth = w_in.shape[0]
    assert all(seq % tile == 0 for tile in SEQ_TILE.values()) and (bsz * seq) % FFN_TILE == 0

    p_lb = jax.nn.softmax(hgrn_lb_logits.astype(_F32), axis=0)
    lower_bounds = jnp.cumsum(p_lb, axis=0) - p_lb[0]
    half = HEAD_DIM // 2
    inv_freq = ROPE_BASE ** (-jnp.linspace(0.0, 1.0, half, dtype=_F32))
    inv_freq2 = jnp.concatenate([inv_freq, inv_freq]).reshape(1, HEAD_DIM)
    pos3 = positions.reshape(bsz, seq, 1)

    for i in range(depth):
        j = i // N_MIXERS
        common = (i, x, mem, norm_mix[i].reshape(1, D_MODEL), w_in, w_out, norm_mem[i].reshape(1, D_MODEL), w_mem_kv)
        if i % N_MIXERS == 0:
            extra = (lower_bounds[j].reshape(1, MIX_WIDTH), hgrn_out_norm[j].reshape(1, MIX_WIDTH))
            x = _mixer_layer("hgrn", *common, extra)
        else:
            extra = (pos3, inv_freq2, ret_out_norm[j].reshape(1, MIX_WIDTH))
            x = _mixer_layer("retention", *common, extra)
        x = _ffn_layer(
            x.reshape(bsz * seq, D_MODEL), i, norm_ffn[i].reshape(1, D_MODEL), w_ffn_in, w_ffn_out,
            norm_final.reshape(1, D_MODEL) if i == depth - 1 else None,
        ).reshape(bsz, seq, D_MODEL)
    return x
```

```python
import functools
import math

import jax
import jax.numpy as jnp
from jax import lax
from jax.experimental import pallas as pl
from jax.experimental.pallas import tpu as pltpu

D_MODEL = 1024
MIX_HEADS = 6
HEAD_DIM = 128
MIX_WIDTH = MIX_HEADS * HEAD_DIM
N_XATTN_HEADS = 4
XATTN_HEAD_DIM = 64
XATTN_WIDTH = N_XATTN_HEADS * XATTN_HEAD_DIM
N_MEM = 256
IN_WIDTH = 4 * MIX_WIDTH + XATTN_WIDTH
CAT_WIDTH = MIX_WIDTH + XATTN_WIDTH
D_FF = 2816
N_MIXERS = 2
ROPE_BASE = 10000.0
EPS = 1e-6
EXP_CLAMP = 30.0

SEQ_TILE = {"hgrn": 1024, "retention": 1024}
CHUNK_SLOTS = {"hgrn": 2, "retention": 1}
CHUNK = {"hgrn": 256, "retention": 512}
PROJ_COLS = 256
HGRN_BLOCK = 128
HGRN_SUB = 32
N_SUB = HGRN_BLOCK // HGRN_SUB
RET_BLOCK = 256
FFN_TILE = 1024
MXU_WIDTH = 256
FFN_CHUNKS = (MXU_WIDTH,) * 11
VMEM_LIMIT_BYTES = 60 * 1024 * 1024

_BF16 = jnp.bfloat16
_F32 = jnp.float32


def _dot(a, b):
    return jnp.dot(a, b, preferred_element_type=_F32)


def _dot_nt(a, b):
    return lax.dot_general(a, b, (((1,), (1,)), ((), ())), preferred_element_type=_F32)


def _dot_tn(a, b):
    return lax.dot_general(a, b, (((0,), (0,)), ((), ())), preferred_element_type=_F32)


def _group_index(idx, group):
    shift = group.bit_length() - 1
    assert 1 << shift == group
    return lax.shift_right_logical(idx, jnp.int32(shift))


def _rmsnorm(x, w):
    return x * lax.rsqrt(jnp.mean(x * x, axis=-1, keepdims=True) + EPS) * w


def _head_lanes(h):
    return slice(h * HEAD_DIM, (h + 1) * HEAD_DIM)


def _log_gamma(h):
    return math.log(1.0 - 2.0 ** (-5.0 - h))


def _memory_kv(mem_ref, nmem_ref, wkv_ref, mkT_ref, mv_ref):
    memh = _rmsnorm(mem_ref[0], nmem_ref[...])
    mkv = _dot(memh, wkv_ref[...])
    mkT_ref[...] = (mkv[:, :XATTN_WIDTH] * (XATTN_HEAD_DIM ** -0.5)).T.astype(_BF16)
    mv = mkv[:, XATTN_WIDTH:]
    lane_head = _group_index(lax.broadcasted_iota(jnp.int32, (N_MEM, XATTN_WIDTH), 1), XATTN_HEAD_DIM)
    for h in range(N_XATTN_HEADS):
        mv_ref[h * N_MEM:(h + 1) * N_MEM, :] = jnp.where(lane_head == h, mv, 0.0).astype(_BF16)


def _cross_attention_items(z_ref, rows, mkT_ref, mv_ref, cat_ref):
    ctx = {}

    def scores():
        q = z_ref[rows, 4 * MIX_WIDTH:].astype(_BF16)
        lane_head = _group_index(lax.broadcasted_iota(jnp.int32, q.shape, 1), XATTN_HEAD_DIM)
        q_heads = [jnp.where(lane_head == h, q, jnp.zeros_like(q)) for h in range(N_XATTN_HEADS)]
        ctx["s"] = _dot(jnp.concatenate(q_heads, axis=0), mkT_ref[...])

    def softmax(h):
        def run():
            r = ctx["s"].shape[0] // N_XATTN_HEADS
            s = ctx["s"][h * r:(h + 1) * r]
            e = jnp.exp(s - jnp.max(s, axis=-1, keepdims=True))
            ctx[h] = (e / jnp.sum(e, axis=-1, keepdims=True)).astype(_BF16)
        return run

    def outputs():
        p = jnp.concatenate([ctx.pop(h) for h in range(N_XATTN_HEADS)], axis=1)
        cat_ref[rows, MIX_WIDTH:] = _dot(p, mv_ref[...])

    return [scores] + [softmax(h) for h in range(N_XATTN_HEADS)] + [outputs]


def _sub_rows(a, c):
    return a[c * HGRN_SUB:(c + 1) * HGRN_SUB]


def _scale_subs(a, vecs):
    parts = []
    for c, vec in enumerate(vecs):
        part = jnp.zeros((HGRN_SUB, a.shape[1]), _F32) if vec is None else _sub_rows(a, c) * vec
        parts.append(part.astype(_BF16))
    return jnp.concatenate(parts, axis=0)


def _hgrn_items(z_ref, rows, lb_ref, onorm_ref, st_ref, cat_ref):
    n = HGRN_BLOCK
    exp_clamp = math.exp(EXP_CLAMP)
    ops = {}
    outs = {}

    def masks():
        ti = lax.broadcasted_iota(jnp.int32, (n, n), 0)
        si = lax.broadcasted_iota(jnp.int32, (n, n), 1)
        mask_a = (_group_index(ti, HGRN_SUB) == _group_index(si, HGRN_SUB)) & (si <= ti)
        mask_b = _group_index(ti, 2 * HGRN_SUB) == _group_index(si, 2 * HGRN_SUB)
        return mask_a, mask_b, si <= ti

    def gates(h):
        def run():
            hs = _head_lanes(h)
            za = z_ref[rows, hs]
            fr = z_ref[rows, MIX_WIDTH + h * HEAD_DIM:MIX_WIDTH + (h + 1) * HEAD_DIM]
            lb = lb_ref[:, hs]
            q = za / (1.0 + jnp.exp(-za))
            e = jnp.exp(-jnp.abs(fr))
            r = 1.0 / (1.0 + e)
            nonneg = fr >= 0.0
            k = (1.0 - lb) * (jnp.where(nonneg, e, 1.0) * r)
            e_neg = jnp.where(nonneg, e, jnp.minimum(1.0 / e, exp_clamp))
            g = jnp.minimum(fr, 0.0) + jnp.log((1.0 + lb * e_neg) * r)
            g_hi = g.astype(_BF16)
            g_lo = (g - g_hi.astype(_F32)).astype(_BF16)
            lower_ones = jnp.where(masks()[2], 1.0, 0.0).astype(_BF16)
            bb = _dot(lower_ones, jnp.concatenate([g_hi, g_lo], axis=1))
            b = bb[:, :HEAD_DIM] + bb[:, HEAD_DIM:]

            mids = [b[c * HGRN_SUB + HGRN_SUB // 2 - 1:c * HGRN_SUB + HGRN_SUB // 2, :] for c in range(N_SUB)]
            ends = [b[(c + 1) * HGRN_SUB - 1:(c + 1) * HGRN_SUB, :] for c in range(N_SUB)]
            ref = jnp.concatenate([jnp.broadcast_to(m, (HGRN_SUB, HEAD_DIM)) for m in mids], axis=0)
            q_a = q * jnp.exp(b - ref)
            k_a = k * jnp.exp(ref - b)
            ops[h] = dict(
                q_a=q_a.astype(_BF16), k_a=k_a.astype(_BF16),
                q_b=_scale_subs(q_a, [None, jnp.exp(mids[1] - ends[0]), None, jnp.exp(mids[3] - ends[2])]),
                k_b=_scale_subs(k_a, [jnp.exp(ends[0] - mids[0]), None, jnp.exp(ends[2] - mids[2]), None]),
                q_c=_scale_subs(q_a, [None, None, jnp.exp(mids[2] - ends[1]), jnp.exp(mids[3] - ends[1])]),
                k_c=_scale_subs(k_a, [jnp.exp(ends[1] - mids[0]), jnp.exp(ends[1] - mids[1]), None, None]),
                q_o=_scale_subs(q_a, [jnp.exp(m) for m in mids]),
                k_s=_scale_subs(k_a, [jnp.exp(ends[N_SUB - 1] - m) for m in mids]),
                decay=jnp.exp(ends[N_SUB - 1]),
            )
        return run

    def mix(h):
        def run():
            op = ops.pop(h)
            mask_a, mask_b, _ = masks()
            vb = z_ref[rows, 2 * MIX_WIDTH + h * HEAD_DIM:2 * MIX_WIDTH + (h + 1) * HEAD_DIM].astype(_BF16)
            s_a = _dot_nt(op["q_a"], op["k_a"])
            s_b = _dot_nt(op["q_b"], op["k_b"])
            s_c = _dot_nt(op["q_c"], op["k_c"])
            scores = jnp.where(mask_a, s_a, 0.0) + jnp.where(mask_b, s_b, 0.0) + s_c
            st = st_ref[h]
            outs[h] = _dot(scores.astype(_BF16), vb) + _dot_nt(op["q_o"], st.astype(_BF16))
            st_ref[h] = st * op["decay"] + _dot_tn(vb, op["k_s"])
        return run

    def finish():
        ss = None
        for h in range(MIX_HEADS):
            part = jnp.sum(outs[h] * outs[h], axis=-1, keepdims=True)
            ss = part if ss is None else ss + part
        inv = lax.rsqrt(ss * (1.0 / MIX_WIDTH) + EPS)
        for h in range(MIX_HEADS):
            hs = _head_lanes(h)
            zg = z_ref[rows, 3 * MIX_WIDTH + h * HEAD_DIM:3 * MIX_WIDTH + (h + 1) * HEAD_DIM]
            cat_ref[rows, hs] = outs[h] * (inv * onorm_ref[:, hs]) / (1.0 + jnp.exp(-zg))

    heads = range(MIX_HEADS)
    return [gates(h) for h in heads] + [mix(h) for h in heads] + [finish]


def _retention_tables(dec_ref, qdec_ref, kdec_ref):
    n = RET_BLOCK
    scale = HEAD_DIM ** -0.5
    ti = lax.broadcasted_iota(jnp.int32, (n, n), 0)
    si = lax.broadcasted_iota(jnp.int32, (n, n), 1)
    rel = (ti - si).astype(_F32)
    pos = lax.broadcasted_iota(jnp.int32, (n, HEAD_DIM), 0).astype(_F32)
    for h in range(MIX_HEADS):
        lg = math.log(1.0 - 2.0 ** (-5.0 - h))
        dec_ref[h] = jnp.where(ti >= si, jnp.exp(lg * jnp.maximum(rel, 0.0)) * scale, 0.0)
        qdec_ref[h] = jnp.exp(lg * (pos + 1.0))
        kdec_ref[h] = jnp.exp(lg * (n - 1.0 - pos)) * scale


def _retention_items(z_ref, rows, cs_ref, onorm_ref, st_ref, cat_ref, dec_ref, qdec_ref, kdec_ref):
    n = RET_BLOCK

    def head(h):
        def run():
            cos2 = cs_ref[0, rows, :]
            sin2 = cs_ref[1, rows, :]

            def rope(t):
                return t * cos2 + pltpu.roll(t, HEAD_DIM // 2, 1) * sin2

            hs = _head_lanes(h)
            q = rope(z_ref[rows, hs])
            k = rope(z_ref[rows, MIX_WIDTH + h * HEAD_DIM:MIX_WIDTH + (h + 1) * HEAD_DIM])
            v = z_ref[rows, 2 * MIX_WIDTH + h * HEAD_DIM:2 * MIX_WIDTH + (h + 1) * HEAD_DIM]
            zg = z_ref[rows, 3 * MIX_WIDTH + h * HEAD_DIM:3 * MIX_WIDTH + (h + 1) * HEAD_DIM]
            qb = q.astype(_BF16)
            vb = v.astype(_BF16)
            scores = _dot_nt(qb, k.astype(_BF16)) * dec_ref[h]
            st = st_ref[h]
            o = _dot(scores.astype(_BF16), vb) + _dot_nt(qb, st.astype(_BF16)) * qdec_ref[h]
            k_s = k * kdec_ref[h]
            st_ref[h] = st * math.exp(_log_gamma(h) * n) + _dot_tn(vb, k_s.astype(_BF16))
            inv = lax.rsqrt(jnp.mean(o * o, axis=-1, keepdims=True) + EPS)
            cat_ref[rows, hs] = o * (inv * onorm_ref[:, hs]) * (zg / (1.0 + jnp.exp(-zg)))
        return run

    return [head(h) for h in range(MIX_HEADS)]


def _rotary_item(pos_ref, crows, freq_ref, cs_ref):
    def run():
        half_rows = (crows.stop - crows.start) // 2
        pos = pos_ref[0, crows, :].astype(_F32)
        lane = lax.broadcasted_iota(jnp.int32, (half_rows, HEAD_DIM), 1)
        low = lane < HEAD_DIM // 2
        ang = jnp.where(low, pos[:half_rows], pos[half_rows:]) * freq_ref[...]
        cos, sin = jnp.cos(ang), jnp.sin(ang)
        cos_sw, sin_sw = pltpu.roll(cos, HEAD_DIM // 2, 1), pltpu.roll(sin, HEAD_DIM // 2, 1)
        cs_ref[0, :half_rows, :] = jnp.where(low, cos, cos_sw)
        cs_ref[0, half_rows:, :] = jnp.where(low, cos_sw, cos)
        cs_ref[1, :half_rows, :] = jnp.where(low, -sin, sin_sw)
        cs_ref[1, half_rows:, :] = jnp.where(low, -sin_sw, sin)
    return run


def _in_proj_items(x_ref, crows, nmix_ref, win_ref, hn_ref, z_ref):
    def norm():
        hn_ref[...] = _rmsnorm(x_ref[0, crows, :], nmix_ref[...])

    def cols(p):
        def run():
            cs = slice(p * PROJ_COLS, (p + 1) * PROJ_COLS)
            z_ref[:, cs] = _dot(hn_ref[...], win_ref[:, cs])
        return run

    return [norm] + [cols(p) for p in range(IN_WIDTH // PROJ_COLS)]


def _out_proj_items(x_ref, crows, cat_ref, wout_ref, out_ref):
    def cols(p):
        def run():
            cs = slice(p * PROJ_COLS, (p + 1) * PROJ_COLS)
            out_ref[0, crows, cs] = x_ref[0, crows, cs] + _dot(cat_ref[...], wout_ref[:, cs])
        return run

    return [cols(p) for p in range(D_MODEL // PROJ_COLS)]


def _run_interleaved(main, filler):
    done = 0
    for k, item in enumerate(main):
        item()
        upto = (k + 1) * len(filler) // len(main)
        for f in filler[done:upto]:
            f()
        done = upto
    for f in filler[done:]:
        f()


def _mixer_kernel(*refs, kind):
    if kind == "hgrn":
        (x_ref, mem_ref, nmix_ref, win_ref, wout_ref, nmem_ref, wkv_ref, lb_ref, onorm_ref,
         out_ref, z_ref, hn_ref, cat_ref, st_ref, mkT_ref, mv_ref) = refs
    else:
        (x_ref, mem_ref, nmix_ref, win_ref, wout_ref, nmem_ref, wkv_ref, pos_ref, freq_ref, onorm_ref,
         out_ref, z_ref, hn_ref, cat_ref, st_ref, mkT_ref, mv_ref, cs_ref, dec_ref, qdec_ref, kdec_ref) = refs

    @pl.when(pl.program_id(1) == 0)
    def _start_of_sequence():
        st_ref[...] = jnp.zeros_like(st_ref)
        _memory_kv(mem_ref, nmem_ref, wkv_ref, mkT_ref, mv_ref)
        if kind != "hgrn":
            _retention_tables(dec_ref, qdec_ref, kdec_ref)

    chunk = CHUNK[kind]
    n_chunks = SEQ_TILE[kind] // chunk
    n_slots = z_ref.shape[0]

    def chunk_rows(c):
        return slice(c * chunk, (c + 1) * chunk)

    def stage_in(c):
        return _in_proj_items(x_ref, chunk_rows(c), nmix_ref, win_ref, hn_ref, z_ref.at[c % n_slots])

    def stage_mix(c):
        zc, cc = z_ref.at[c % n_slots], cat_ref.at[c % n_slots]
        whole = slice(0, chunk)
        xattn = _cross_attention_items(zc, whole, mkT_ref, mv_ref, cc)
        if kind == "hgrn":
            items = xattn
            for i in range(chunk // HGRN_BLOCK):
                rows = slice(i * HGRN_BLOCK, (i + 1) * HGRN_BLOCK)
                items += _hgrn_items(zc, rows, lb_ref, onorm_ref, st_ref, cc)
            return items
        items = []
        for i in range(chunk // RET_BLOCK):
            rows = slice(i * RET_BLOCK, (i + 1) * RET_BLOCK)
            items += _retention_items(zc, rows, cs_ref.at[c], onorm_ref, st_ref, cc, dec_ref, qdec_ref, kdec_ref)
        return items + xattn

    def stage_out(c):
        return _out_proj_items(x_ref, chunk_rows(c), cat_ref.at[c % n_slots], wout_ref, out_ref)

    if kind != "hgrn":
        for c in range(n_chunks):
            _rotary_item(pos_ref, chunk_rows(c), freq_ref, cs_ref.at[c])()
    if n_slots == 1:
        for c in range(n_chunks):
            for item in stage_in(c) + stage_mix(c) + stage_out(c):
                item()
        return
    for item in stage_in(0):
        item()
    for c in range(n_chunks):
        filler = (stage_out(c - 1) if c > 0 else []) + (stage_in(c + 1) if c + 1 < n_chunks else [])
        _run_interleaved(stage_mix(c), filler)
    for item in stage_out(n_chunks - 1):
        item()


def _mixer_layer(kind, layer, x, mem, norm_mix, w_in, w_out, norm_mem, w_mem_kv, extra):
    bsz, seq, _ = x.shape
    seq_tile = SEQ_TILE[kind]
    const2 = lambda b, t: (0, 0)
    layer3 = lambda b, t: (layer, 0, 0)
    in_specs = [
        pl.BlockSpec((1, seq_tile, D_MODEL), lambda b, t: (b, t, 0)),
        pl.BlockSpec((1, N_MEM, D_MODEL), lambda b, t: (b, 0, 0)),
        pl.BlockSpec((1, D_MODEL), const2),
        pl.BlockSpec((None, D_MODEL, IN_WIDTH), layer3, pipeline_mode=pl.Buffered(1)),
        pl.BlockSpec((None, CAT_WIDTH, D_MODEL), layer3, pipeline_mode=pl.Buffered(1)),
        pl.BlockSpec((1, D_MODEL), const2),
        pl.BlockSpec((None, D_MODEL, 2 * XATTN_WIDTH), layer3, pipeline_mode=pl.Buffered(1)),
    ]
    chunk = CHUNK[kind]
    n_slots = min(CHUNK_SLOTS[kind], seq_tile // chunk)
    scratch = [
        pltpu.VMEM((n_slots, chunk, IN_WIDTH), _F32),
        pltpu.VMEM((chunk, D_MODEL), _F32),
        pltpu.VMEM((n_slots, chunk, CAT_WIDTH), _F32),
        pltpu.VMEM((MIX_HEADS, HEAD_DIM, HEAD_DIM), _F32),
        pltpu.VMEM((XATTN_WIDTH, N_MEM), _BF16),
        pltpu.VMEM((N_XATTN_HEADS * N_MEM, XATTN_WIDTH), _BF16),
    ]
    if kind == "hgrn":
        in_specs += [pl.BlockSpec((1, MIX_WIDTH), const2), pl.BlockSpec((1, MIX_WIDTH), const2)]
    else:
        in_specs += [
            pl.BlockSpec((1, seq_tile, 1), lambda b, t: (b, t, 0)),
            pl.BlockSpec((1, HEAD_DIM), const2),
            pl.BlockSpec((1, MIX_WIDTH), const2),
        ]
        scratch += [
            pltpu.VMEM((seq_tile // chunk, 2, chunk, HEAD_DIM), _F32),
            pltpu.VMEM((MIX_HEADS, RET_BLOCK, RET_BLOCK), _F32),
            pltpu.VMEM((MIX_HEADS, RET_BLOCK, HEAD_DIM), _F32),
            pltpu.VMEM((MIX_HEADS, RET_BLOCK, HEAD_DIM), _F32),
        ]
    return pl.pallas_call(
        functools.partial(_mixer_kernel, kind=kind),
        grid=(bsz, seq // seq_tile),
        in_specs=in_specs,
        out_specs=pl.BlockSpec((1, seq_tile, D_MODEL), lambda b, t: (b, t, 0)),
        out_shape=jax.ShapeDtypeStruct(x.shape, x.dtype),
        scratch_shapes=scratch,
        compiler_params=pltpu.CompilerParams(
            dimension_semantics=("arbitrary", "arbitrary"), vmem_limit_bytes=VMEM_LIMIT_BYTES),
        name=f"{kind}_mixer_layer",
    )(x, mem, norm_mix, w_in, w_out, norm_mem, w_mem_kv, *extra)


def _ffn_kernel(*refs, final):
    if final:
        x_ref, n_ref, win_ref, wout_ref, nf_ref, out_ref = refs
    else:
        x_ref, n_ref, win_ref, wout_ref, out_ref = refs
    x = x_ref[...]
    hn = _rmsnorm(x, n_ref[...])
    acc = x
    assert sum(FFN_CHUNKS) == D_FF
    lo = 0
    for width in FFN_CHUNKS:
        hi = lo + width
        g = _dot(hn, win_ref[:, lo:hi])
        u = _dot(hn, win_ref[:, D_FF + lo:D_FF + hi])
        a = g * jax.nn.sigmoid(g) * u
        acc = acc + _dot(a, wout_ref[lo:hi, :])
        lo = hi
    if final:
        acc = _rmsnorm(acc, nf_ref[...])
    out_ref[...] = acc


def _ffn_layer(x2d, layer, norm_ffn, w_ffn_in, w_ffn_out, norm_final):
    final = norm_final is not None
    const = lambda i: (0, 0)
    in_specs = [
        pl.BlockSpec((FFN_TILE, D_MODEL), lambda i: (i, 0)),
        pl.BlockSpec((1, D_MODEL), const),
        pl.BlockSpec((None, D_MODEL, 2 * D_FF), lambda i: (layer, 0, 0), pipeline_mode=pl.Buffered(1)),
        pl.BlockSpec((None, D_FF, D_MODEL), lambda i: (layer, 0, 0), pipeline_mode=pl.Buffered(1)),
    ]
    args = [x2d, norm_ffn, w_ffn_in, w_ffn_out]
    if final:
        in_specs.append(pl.BlockSpec((1, D_MODEL), const))
        args.append(norm_final)
    return pl.pallas_call(
        functools.partial(_ffn_kernel, final=final),
        grid=(x2d.shape[0] // FFN_TILE,),
        in_specs=in_specs,
        out_specs=pl.BlockSpec((FFN_TILE, D_MODEL), lambda i: (i, 0)),
        out_shape=jax.ShapeDtypeStruct(x2d.shape, x2d.dtype),
        compiler_params=pltpu.CompilerParams(
            dimension_semantics=("arbitrary",), vmem_limit_bytes=VMEM_LIMIT_BYTES),
        name="swiglu_ffn_final" if final else "swiglu_ffn",
    )(*args)


def kernel(x, mem, positions, norm_mix, w_in, w_out, norm_mem, w_mem_kv, hgrn_lb_logits, hgrn_out_norm,
           ret_out_norm, norm_ffn, w_ffn_in, w_ffn_out, norm_final):
    bsz, seq, _ = x.shape
    depth = w_in.shape[0]
    assert all(seq % tile == 0 for tile in SEQ_TILE.values()) and (bsz * seq) % FFN_TILE == 0

    p_lb = jax.nn.softmax(hgrn_lb_logits.astype(_F32), axis=0)
    lower_bounds = jnp.cumsum(p_lb, axis=0) - p_lb[0]
    half = HEAD_DIM // 2
    inv_freq = ROPE_BASE ** (-jnp.linspace(0.0, 1.0, half, dtype=_F32))
    inv_freq2 = jnp.concatenate([inv_freq, inv_freq]).reshape(1, HEAD_DIM)
    pos3 = positions.reshape(bsz, seq, 1)

    for i in range(depth):
        j = i // N_MIXERS
        common = (i, x, mem, norm_mix[i].reshape(1, D_MODEL), w_in, w_out, norm_mem[i].reshape(1, D_MODEL), w_mem_kv)
        if i % N_MIXERS == 0:
            extra = (lower_bounds[j].reshape(1, MIX_WIDTH), hgrn_out_norm[j].reshape(1, MIX_WIDTH))
            x = _mixer_layer("hgrn", *common, extra)
        else:
            extra = (pos3, inv_freq2, ret_out_norm[j].reshape(1, MIX_WIDTH))
            x = _mixer_layer("retention", *common, extra)
        x = _ffn_layer(
            x.reshape(bsz * seq, D_MODEL), i, norm_ffn[i].reshape(1, D_MODEL), w_ffn_in, w_ffn_out,
            norm_final.reshape(1, D_MODEL) if i == depth - 1 else None,
        ).reshape(bsz, seq, D_MODEL)
    return x
```

```python
import functools
import math

import jax
import jax.numpy as jnp
from jax import lax
from jax.experimental import pallas as pl
from jax.experimental.pallas import tpu as pltpu

D_MODEL = 1024
MIX_HEADS = 6
HEAD_DIM = 128
MIX_WIDTH = MIX_HEADS * HEAD_DIM
N_XATTN_HEADS = 4
XATTN_HEAD_DIM = 64
XATTN_WIDTH = N_XATTN_HEADS * XATTN_HEAD_DIM
N_MEM = 256
IN_WIDTH = 4 * MIX_WIDTH + XATTN_WIDTH
CAT_WIDTH = MIX_WIDTH + XATTN_WIDTH
D_FF = 2816
N_MIXERS = 2
ROPE_BASE = 10000.0
EPS = 1e-6
EXP_CLAMP = 30.0

SEQ_TILE = {"hgrn": 1024, "retention": 1024}
CHUNK_SLOTS = {"hgrn": 2, "retention": 1}
CHUNK = {"hgrn": 256, "retention": 512}
PROJ_COLS = 256
HGRN_BLOCK = 128
HGRN_SUB = 32
N_SUB = HGRN_BLOCK // HGRN_SUB
RET_BLOCK = 256
FFN_TILE = 1024
MXU_WIDTH = 256
FFN_CHUNKS = (MXU_WIDTH,) * 11
VMEM_LIMIT_BYTES = 60 * 1024 * 1024

_BF16 = jnp.bfloat16
_F32 = jnp.float32


def _dot(a, b):
    return jnp.dot(a, b, preferred_element_type=_F32)


def _dot_nt(a, b):
    return lax.dot_general(a, b, (((1,), (1,)), ((), ())), preferred_element_type=_F32)


def _dot_tn(a, b):
    return lax.dot_general(a, b, (((0,), (0,)), ((), ())), preferred_element_type=_F32)


def _group_index(idx, group):
    shift = group.bit_length() - 1
    assert 1 << shift == group
    return lax.shift_right_logical(idx, jnp.int32(shift))


def _rmsnorm(x, w):
    return x * lax.rsqrt(jnp.mean(x * x, axis=-1, keepdims=True) + EPS) * w


def _head_lanes(h):
    return slice(h * HEAD_DIM, (h + 1) * HEAD_DIM)


def _log_gamma(h):
    return math.log(1.0 - 2.0 ** (-5.0 - h))


def _memory_kv(mem_ref, nmem_ref, wkv_ref, mkT_ref, mv_ref):
    memh = _rmsnorm(mem_ref[0], nmem_ref[...])
    mkv = _dot(memh, wkv_ref[...])
    mkT_ref[...] = (mkv[:, :XATTN_WIDTH] * (XATTN_HEAD_DIM ** -0.5)).T.astype(_BF16)
    mv = mkv[:, XATTN_WIDTH:]
    lane_head = _group_index(lax.broadcasted_iota(jnp.int32, (N_MEM, XATTN_WIDTH), 1), XATTN_HEAD_DIM)
    for h in range(N_XATTN_HEADS):
        mv_ref[h * N_MEM:(h + 1) * N_MEM, :] = jnp.where(lane_head == h, mv, 0.0).astype(_BF16)


def _cross_attention_items(z_ref, rows, mkT_ref, mv_ref, cat_ref):
    ctx = {}

    def scores():
        q = z_ref[rows, 4 * MIX_WIDTH:].astype(_BF16)
        lane_head = _group_index(lax.broadcasted_iota(jnp.int32, q.shape, 1), XATTN_HEAD_DIM)
        q_heads = [jnp.where(lane_head == h, q, jnp.zeros_like(q)) for h in range(N_XATTN_HEADS)]
        ctx["s"] = _dot(jnp.concatenate(q_heads, axis=0), mkT_ref[...])

    def softmax(h):
        def run():
            r = ctx["s"].shape[0] // N_XATTN_HEADS
            s = ctx["s"][h * r:(h + 1) * r]
            e = jnp.exp(s - jnp.max(s, axis=-1, keepdims=True))
            ctx[h] = (e / jnp.sum(e, axis=-1, keepdims=True)).astype(_BF16)
        return run

    def outputs():
        p = jnp.concatenate([ctx.pop(h) for h in range(N_XATTN_HEADS)], axis=1)
        cat_ref[rows, MIX_WIDTH:] = _dot(p, mv_ref[...])

    return [scores] + [softmax(h) for h in range(N_XATTN_HEADS)] + [outputs]


def _sub_rows(a, c):
    return a[c * HGRN_SUB:(c + 1) * HGRN_SUB]


def _scale_subs(a, vecs):
    parts = []
    for c, vec in enumerate(vecs):
        part = jnp.zeros((HGRN_SUB, a.shape[1]), _F32) if vec is None else _sub_rows(a, c) * vec
        parts.append(part.astype(_BF16))
    return jnp.concatenate(parts, axis=0)


def _hgrn_items(z_ref, rows, lb_ref, onorm_ref, st_ref, cat_ref):
    n = HGRN_BLOCK
    exp_clamp = math.exp(EXP_CLAMP)
    ops = {}
    outs = {}

    def masks():
        ti = lax.broadcasted_iota(jnp.int32, (n, n), 0)
        si = lax.broadcasted_iota(jnp.int32, (n, n), 1)
        mask_a = (_group_index(ti, HGRN_SUB) == _group_index(si, HGRN_SUB)) & (si <= ti)
        mask_b = _group_index(ti, 2 * HGRN_SUB) == _group_index(si, 2 * HGRN_SUB)
        return mask_a, mask_b, si <= ti

    def gates(h):
        def run():
            hs = _head_lanes(h)
            za = z_ref[rows, hs]
            fr = z_ref[rows, MIX_WIDTH + h * HEAD_DIM:MIX_WIDTH + (h + 1) * HEAD_DIM]
            lb = lb_ref[:, hs]
            q = za / (1.0 + jnp.exp(-za))
            e = jnp.exp(-jnp.abs(fr))
            r = 1.0 / (1.0 + e)
            nonneg = fr >= 0.0
            k = (1.0 - lb) * (jnp.where(nonneg, e, 1.0) * r)
            e_neg = jnp.where(nonneg, e, jnp.minimum(1.0 / e, exp_clamp))
            g = jnp.minimum(fr, 0.0) + jnp.log((1.0 + lb * e_neg) * r)
            g_hi = g.astype(_BF16)
            g_lo = (g - g_hi.astype(_F32)).astype(_BF16)
            lower_ones = jnp.where(masks()[2], 1.0, 0.0).astype(_BF16)
            bb = _dot(lower_ones, jnp.concatenate([g_hi, g_lo], axis=1))
            b = bb[:, :HEAD_DIM] + bb[:, HEAD_DIM:]

            mids = [b[c * HGRN_SUB + HGRN_SUB // 2 - 1:c * HGRN_SUB + HGRN_SUB // 2, :] for c in range(N_SUB)]
            ends = [b[(c + 1) * HGRN_SUB - 1:(c + 1) * HGRN_SUB, :] for c in range(N_SUB)]
            ref = jnp.concatenate([jnp.broadcast_to(m, (HGRN_SUB, HEAD_DIM)) for m in mids], axis=0)
            q_a = q * jnp.exp(b - ref)
            k_a = k * jnp.exp(ref - b)
            ops[h] = dict(
                q_a=q_a.astype(_BF16), k_a=k_a.astype(_BF16),
                q_b=_scale_subs(q_a, [None, jnp.exp(mids[1] - ends[0]), None, jnp.exp(mids[3] - ends[2])]),
                k_b=_scale_subs(k_a, [jnp.exp(ends[0] - mids[0]), None, jnp.exp(ends[2] - mids[2]), None]),
                q_c=_scale_subs(q_a, [None, None, jnp.exp(mids[2] - ends[1]), jnp.exp(mids[3] - ends[1])]),
                k_c=_scale_subs(k_a, [jnp.exp(ends[1] - mids[0]), jnp.exp(ends[1] - mids[1]), None, None]),
                q_o=_scale_subs(q_a, [jnp.exp(m) for m in mids]),
                k_s=_scale_subs(k_a, [jnp.exp(ends[N_SUB - 1] - m) for m in mids]),
                decay=jnp.exp(ends[N_SUB - 1]),
            )
        return run

    def mix(h):
        def run():
            op = ops.pop(h)
            mask_a, mask_b, _ = masks()
            vb = z_ref[rows, 2 * MIX_WIDTH + h * HEAD_DIM:2 * MIX_WIDTH + (h + 1) * HEAD_DIM].astype(_BF16)
            s_a = _dot_nt(op["q_a"], op["k_a"])
            s_b = _dot_nt(op["q_b"], op["k_b"])
            s_c = _dot_nt(op["q_c"], op["k_c"])
            scores = jnp.where(mask_a, s_a, 0.0) + jnp.where(mask_b, s_b, 0.0) + s_c
            st = st_ref[h]
            outs[h] = _dot(scores.astype(_BF16), vb) + _dot_nt(op["q_o"], st.astype(_BF16))
            st_ref[h] = st * op["decay"] + _dot_tn(vb, op["k_s"])
        return run

    def finish():
        ss = None
        for h in range(MIX_HEADS):
            part = jnp.sum(outs[h] * outs[h], axis=-1, keepdims=True)
            ss = part if ss is None else ss + part
        inv = lax.rsqrt(ss * (1.0 / MIX_WIDTH) + EPS)
        for h in range(MIX_HEADS):
            hs = _head_lanes(h)
            zg = z_ref[rows, 3 * MIX_WIDTH + h * HEAD_DIM:3 * MIX_WIDTH + (h + 1) * HEAD_DIM]
            cat_ref[rows, hs] = outs[h] * (inv * onorm_ref[:, hs]) / (1.0 + jnp.exp(-zg))

    heads = range(MIX_HEADS)
    return [gates(h) for h in heads] + [mix(h) for h in heads] + [finish]


def _retention_tables(dec_ref, qdec_ref, kdec_ref):
    n = RET_BLOCK
    scale = HEAD_DIM ** -0.5
    ti = lax.broadcasted_iota(jnp.int32, (n, n), 0)
    si = lax.broadcasted_iota(jnp.int32, (n, n), 1)
    rel = (ti - si).astype(_F32)
    pos = lax.broadcasted_iota(jnp.int32, (n, HEAD_DIM), 0).astype(_F32)
    for h in range(MIX_HEADS):
        lg = math.log(1.0 - 2.0 ** (-5.0 - h))
        dec_ref[h] = jnp.where(ti >= si, jnp.exp(lg * jnp.maximum(rel, 0.0)) * scale, 0.0)
        qdec_ref[h] = jnp.exp(lg * (pos + 1.0))
        kdec_ref[h] = jnp.exp(lg * (n - 1.0 - pos)) * scale


def _retention_items(z_ref, rows, cs_ref, onorm_ref, st_ref, cat_ref, dec_ref, qdec_ref, kdec_ref):
    n = RET_BLOCK

    def head(h):
        def run():
            cos2 = cs_ref[0, rows, :]
            sin2 = cs_ref[1, rows, :]

            def rope(t):
                return t * cos2 + pltpu.roll(t, HEAD_DIM // 2, 1) * sin2

            hs = _head_lanes(h)
            q = rope(z_ref[rows, hs])
            k = rope(z_ref[rows, MIX_WIDTH + h * HEAD_DIM:MIX_WIDTH + (h + 1) * HEAD_DIM])
            v = z_ref[rows, 2 * MIX_WIDTH + h * HEAD_DIM:2 * MIX_WIDTH + (h + 1) * HEAD_DIM]
            zg = z_ref[rows, 3 * MIX_WIDTH + h * HEAD_DIM:3 * MIX_WIDTH + (h + 1) * HEAD_DIM]
            qb = q.astype(_BF16)
            vb = v.astype(_BF16)
            scores = _dot_nt(qb, k.astype(_BF16)) * dec_ref[h]
            st = st_ref[h]
            o = _dot(scores.astype(_BF16), vb) + _dot_nt(qb, st.astype(_BF16)) * qdec_ref[h]
            k_s = k * kdec_ref[h]
            st_ref[h] = st * math.exp(_log_gamma(h) * n) + _dot_tn(vb, k_s.astype(_BF16))
            inv = lax.rsqrt(jnp.mean(o * o, axis=-1, keepdims=True) + EPS)
            cat_ref[rows, hs] = o * (inv * onorm_ref[:, hs]) * (zg / (1.0 + jnp.exp(-zg)))
        return run

    return [head(h) for h in range(MIX_HEADS)]


def _rotary_item(pos_ref, crows, freq_ref, cs_ref):
    def run():
        half_rows = (crows.stop - crows.start) // 2
        pos = pos_ref[0, crows, :].astype(_F32)
        lane = lax.broadcasted_iota(jnp.int32, (half_rows, HEAD_DIM), 1)
        low = lane < HEAD_DIM // 2
        ang = jnp.where(low, pos[:half_rows], pos[half_rows:]) * freq_ref[...]
        cos, sin = jnp.cos(ang), jnp.sin(ang)
        cos_sw, sin_sw = pltpu.roll(cos, HEAD_DIM // 2, 1), pltpu.roll(sin, HEAD_DIM // 2, 1)
        cs_ref[0, :half_rows, :] = jnp.where(low, cos, cos_sw)
        cs_ref[0, half_rows:, :] = jnp.where(low, cos_sw, cos)
        cs_ref[1, :half_rows, :] = jnp.where(low, -sin, sin_sw)
        cs_ref[1, half_rows:, :] = jnp.where(low, -sin_sw, sin)
    return run


def _in_proj_items(x_ref, crows, nmix_ref, win_ref, hn_ref, z_ref):
    def norm():
        hn_ref[...] = _rmsnorm(x_ref[0, crows, :], nmix_ref[...])

    def cols(p):
        def run():
            cs = slice(p * PROJ_COLS, (p + 1) * PROJ_COLS)
            z_ref[:, cs] = _dot(hn_ref[...], win_ref[:, cs])
        return run

    return [norm] + [cols(p) for p in range(IN_WIDTH // PROJ_COLS)]


def _out_proj_items(x_ref, crows, cat_ref, wout_ref, out_ref):
    def cols(p):
        def run():
            cs = slice(p * PROJ_COLS, (p + 1) * PROJ_COLS)
            out_ref[0, crows, cs] = x_ref[0, crows, cs] + _dot(cat_ref[...], wout_ref[:, cs])
        return run

    return [cols(p) for p in range(D_MODEL // PROJ_COLS)]


def _run_interleaved(main, filler):
    done = 0
    for k, item in enumerate(main):
        item()
        upto = (k + 1) * len(filler) // len(main)
        for f in filler[done:upto]:
            f()
        done = upto
    for f in filler[done:]:
        f()


def _mixer_kernel(*refs, kind):
    if kind == "hgrn":
        (x_ref, mem_ref, nmix_ref, win_ref, wout_ref, nmem_ref, wkv_ref, lb_ref, onorm_ref,
         out_ref, z_ref, hn_ref, cat_ref, st_ref, mkT_ref, mv_ref) = refs
    else:
        (x_ref, mem_ref, nmix_ref, win_ref, wout_ref, nmem_ref, wkv_ref, pos_ref, freq_ref, onorm_ref,
         out_ref, z_ref, hn_ref, cat_ref, st_ref, mkT_ref, mv_ref, cs_ref, dec_ref, qdec_ref, kdec_ref) = refs

    @pl.when(pl.program_id(1) == 0)
    def _start_of_sequence():
        st_ref[...] = jnp.zeros_like(st_ref)
        _memory_kv(mem_ref, nmem_ref, wkv_ref, mkT_ref, mv_ref)
        if kind != "hgrn":
            _retention_tables(dec_ref, qdec_ref, kdec_ref)

    chunk = CHUNK[kind]
    n_chunks = SEQ_TILE[kind] // chunk
    n_slots = z_ref.shape[0]

    def chunk_rows(c):
        return slice(c * chunk, (c + 1) * chunk)

    def stage_in(c):
        return _in_proj_items(x_ref, chunk_rows(c), nmix_ref, win_ref, hn_ref, z_ref.at[c % n_slots])

    def stage_mix(c):
        zc, cc = z_ref.at[c % n_slots], cat_ref.at[c % n_slots]
        whole = slice(0, chunk)
        xattn = _cross_attention_items(zc, whole, mkT_ref, mv_ref, cc)
        if kind == "hgrn":
            items = []
            for i in range(chunk // HGRN_BLOCK):
                rows = slice(i * HGRN_BLOCK, (i + 1) * HGRN_BLOCK)
                items += _hgrn_items(zc, rows, lb_ref, onorm_ref, st_ref, cc)
            return items + xattn
        items = []
        for i in range(chunk // RET_BLOCK):
            rows = slice(i * RET_BLOCK, (i + 1) * RET_BLOCK)
            items += _retention_items(zc, rows, cs_ref.at[c], onorm_ref, st_ref, cc, dec_ref, qdec_ref, kdec_ref)
        return items + xattn

    def stage_out(c):
        return _out_proj_items(x_ref, chunk_rows(c), cat_ref.at[c % n_slots], wout_ref, out_ref)

    if kind != "hgrn":
        for c in range(n_chunks):
            _rotary_item(pos_ref, chunk_rows(c), freq_ref, cs_ref.at[c])()
    if n_slots == 1:
        for c in range(n_chunks):
            for item in stage_in(c) + stage_mix(c) + stage_out(c):
                item()
        return
    for item in stage_in(0):
        item()
    for c in range(n_chunks):
        filler = (stage_out(c - 1) if c > 0 else []) + (stage_in(c + 1) if c + 1 < n_chunks else [])
        _run_interleaved(stage_mix(c), filler)
    for item in stage_out(n_chunks - 1):
        item()


def _mixer_layer(kind, layer, x, mem, norm_mix, w_in, w_out, norm_mem, w_mem_kv, extra):
    bsz, seq, _ = x.shape
    seq_tile = SEQ_TILE[kind]
    const2 = lambda b, t: (0, 0)
    layer3 = lambda b, t: (layer, 0, 0)
    in_specs = [
        pl.BlockSpec((1, seq_tile, D_MODEL), lambda b, t: (b, t, 0)),
        pl.BlockSpec((1, N_MEM, D_MODEL), lambda b, t: (b, 0, 0)),
        pl.BlockSpec((1, D_MODEL), const2),
        pl.BlockSpec((None, D_MODEL, IN_WIDTH), layer3, pipeline_mode=pl.Buffered(1)),
        pl.BlockSpec((None, CAT_WIDTH, D_MODEL), layer3, pipeline_mode=pl.Buffered(1)),
        pl.BlockSpec((1, D_MODEL), const2),
        pl.BlockSpec((None, D_MODEL, 2 * XATTN_WIDTH), layer3, pipeline_mode=pl.Buffered(1)),
    ]
    chunk = CHUNK[kind]
    n_slots = min(CHUNK_SLOTS[kind], seq_tile // chunk)
    scratch = [
        pltpu.VMEM((n_slots, chunk, IN_WIDTH), _F32),
        pltpu.VMEM((chunk, D_MODEL), _F32),
        pltpu.VMEM((n_slots, chunk, CAT_WIDTH), _F32),
        pltpu.VMEM((MIX_HEADS, HEAD_DIM, HEAD_DIM), _F32),
        pltpu.VMEM((XATTN_WIDTH, N_MEM), _BF16),
        pltpu.VMEM((N_XATTN_HEADS * N_MEM, XATTN_WIDTH), _BF16),
    ]
    if kind == "hgrn":
        in_specs += [pl.BlockSpec((1, MIX_WIDTH), const2), pl.BlockSpec((1, MIX_WIDTH), const2)]
    else:
        in_specs += [
            pl.BlockSpec((1, seq_tile, 1), lambda b, t: (b, t, 0)),
            pl.BlockSpec((1, HEAD_DIM), const2),
            pl.BlockSpec((1, MIX_WIDTH), const2),
        ]
        scratch += [
            pltpu.VMEM((seq_tile // chunk, 2, chunk, HEAD_DIM), _F32),
            pltpu.VMEM((MIX_HEADS, RET_BLOCK, RET_BLOCK), _F32),
            pltpu.VMEM((MIX_HEADS, RET_BLOCK, HEAD_DIM), _F32),
            pltpu.VMEM((MIX_HEADS, RET_BLOCK, HEAD_DIM), _F32),
        ]
    return pl.pallas_call(
        functools.partial(_mixer_kernel, kind=kind),
        grid=(bsz, seq // seq_tile),
        in_specs=in_specs,
        out_specs=pl.BlockSpec((1, seq_tile, D_MODEL), lambda b, t: (b, t, 0)),
        out_shape=jax.ShapeDtypeStruct(x.shape, x.dtype),
        scratch_shapes=scratch,
        compiler_params=pltpu.CompilerParams(
            dimension_semantics=("arbitrary", "arbitrary"), vmem_limit_bytes=VMEM_LIMIT_BYTES),
        name=f"{kind}_mixer_layer",
    )(x, mem, norm_mix, w_in, w_out, norm_mem, w_mem_kv, *extra)


def _ffn_kernel(*refs, final):
    if final:
        x_ref, n_ref, win_ref, wout_ref, nf_ref, out_ref = refs
    else:
        x_ref, n_ref, win_ref, wout_ref, out_ref = refs
    x = x_ref[...]
    hn = _rmsnorm(x, n_ref[...])
    acc = x
    assert sum(FFN_CHUNKS) == D_FF
    lo = 0
    for width in FFN_CHUNKS:
        hi = lo + width
        g = _dot(hn, win_ref[:, lo:hi])
        u = _dot(hn, win_ref[:, D_FF + lo:D_FF + hi])
        a = g * jax.nn.sigmoid(g) * u
        acc = acc + _dot(a, wout_ref[lo:hi, :])
        lo = hi
    if final:
        acc = _rmsnorm(acc, nf_ref[...])
    out_ref[...] = acc


def _ffn_layer(x2d, layer, norm_ffn, w_ffn_in, w_ffn_out, norm_final):
    final = norm_final is not None
    const = lambda i: (0, 0)
    in_specs = [
        pl.BlockSpec((FFN_TILE, D_MODEL), lambda i: (i, 0)),
        pl.BlockSpec((1, D_MODEL), const),
        pl.BlockSpec((None, D_MODEL, 2 * D_FF), lambda i: (layer, 0, 0), pipeline_mode=pl.Buffered(1)),
        pl.BlockSpec((None, D_FF, D_MODEL), lambda i: (layer, 0, 0), pipeline_mode=pl.Buffered(1)),
    ]
    args = [x2d, norm_ffn, w_ffn_in, w_ffn_out]
    if final:
        in_specs.append(pl.BlockSpec((1, D_MODEL), const))
        args.append(norm_final)
    return pl.pallas_call(
        functools.partial(_ffn_kernel, final=final),
        grid=(x2d.shape[0] // FFN_TILE,),
        in_specs=in_specs,
        out_specs=pl.BlockSpec((FFN_TILE, D_MODEL), lambda i: (i, 0)),
        out_shape=jax.ShapeDtypeStruct(x2d.shape, x2d.dtype),
        compiler_params=pltpu.CompilerParams(
            dimension_semantics=("arbitrary",), vmem_limit_bytes=VMEM_LIMIT_BYTES),
        name="swiglu_ffn_final" if final else "swiglu_ffn",
    )(*args)


def kernel(x, mem, positions, norm_mix, w_in, w_out, norm_mem, w_mem_kv, hgrn_lb_logits, hgrn_out_norm,
           ret_out_norm, norm_ffn, w_ffn_in, w_ffn_out, norm_final):
    bsz, seq, _ = x.shape
    depth = w_in.shape[0]
    assert all(seq % tile == 0 for tile in SEQ_TILE.values()) and (bsz * seq) % FFN_TILE == 0

    p_lb = jax.nn.softmax(hgrn_lb_logits.astype(_F32), axis=0)
    lower_bounds = jnp.cumsum(p_lb, axis=0) - p_lb[0]
    half = HEAD_DIM // 2
    inv_freq = ROPE_BASE ** (-jnp.linspace(0.0, 1.0, half, dtype=_F32))
    inv_freq2 = jnp.concatenate([inv_freq, inv_freq]).reshape(1, HEAD_DIM)
    pos3 = positions.reshape(bsz, seq, 1)

    for i in range(depth):
        j = i // N_MIXERS
        common = (i, x, mem, norm_mix[i].reshape(1, D_MODEL), w_in, w_out, norm_mem[i].reshape(1, D_MODEL), w_mem_kv)
        if i % N_MIXERS == 0:
            extra = (lower_bounds[j].reshape(1, MIX_WIDTH), hgrn_out_norm[j].reshape(1, MIX_WIDTH))
            x = _mixer_layer("hgrn", *common, extra)
        else:
            extra = (pos3, inv_freq2, ret_out_norm[j].reshape(1, MIX_WIDTH))
            x = _mixer_layer("retention", *common, extra)
        x = _ffn_layer(
            x.reshape(bsz * seq, D_MODEL), i, norm_ffn[i].reshape(1, D_MODEL), w_ffn_in, w_ffn_out,
            norm_final.reshape(1, D_MODEL) if i == depth - 1 else None,
        ).reshape(bsz, seq, D_MODEL)
    return x
```

```python
import functools
import math

import jax
import jax.numpy as jnp
from jax import lax
from jax.experimental import pallas as pl
from jax.experimental.pallas import tpu as pltpu

D_MODEL = 1024
MIX_HEADS = 6
HEAD_DIM = 128
MIX_WIDTH = MIX_HEADS * HEAD_DIM
N_XATTN_HEADS = 4
XATTN_HEAD_DIM = 64
XATTN_WIDTH = N_XATTN_HEADS * XATTN_HEAD_DIM
N_MEM = 256
IN_WIDTH = 4 * MIX_WIDTH + XATTN_WIDTH
CAT_WIDTH = MIX_WIDTH + XATTN_WIDTH
D_FF = 2816
N_MIXERS = 2
ROPE_BASE = 10000.0
EPS = 1e-6
EXP_CLAMP = 30.0

SEQ_TILE = {"hgrn": 1024, "retention": 1024}
CHUNK_SLOTS = {"hgrn": 2, "retention": 1}
CHUNK = {"hgrn": 256, "retention": 512}
PROJ_COLS = 256
HGRN_BLOCK = 128
HGRN_SUB = 32
N_SUB = HGRN_BLOCK // HGRN_SUB
RET_BLOCK = 256
FFN_TILE = 1024
MXU_WIDTH = 256
FFN_CHUNKS = (MXU_WIDTH,) * 11
VMEM_LIMIT_BYTES = 60 * 1024 * 1024

_BF16 = jnp.bfloat16
_F32 = jnp.float32


def _dot(a, b):
    return jnp.dot(a, b, preferred_element_type=_F32)


def _dot_nt(a, b):
    return lax.dot_general(a, b, (((1,), (1,)), ((), ())), preferred_element_type=_F32)


def _dot_tn(a, b):
    return lax.dot_general(a, b, (((0,), (0,)), ((), ())), preferred_element_type=_F32)


def _group_index(idx, group):
    shift = group.bit_length() - 1
    assert 1 << shift == group
    return lax.shift_right_logical(idx, jnp.int32(shift))


def _rmsnorm(x, w):
    return x * lax.rsqrt(jnp.mean(x * x, axis=-1, keepdims=True) + EPS) * w


def _head_lanes(h):
    return slice(h * HEAD_DIM, (h + 1) * HEAD_DIM)


def _log_gamma(h):
    return math.log(1.0 - 2.0 ** (-5.0 - h))


def _memory_kv(mem_ref, nmem_ref, wkv_ref, mkT_ref, mv_ref):
    memh = _rmsnorm(mem_ref[0], nmem_ref[...])
    mkv = _dot(memh, wkv_ref[...])
    mkT_ref[...] = (mkv[:, :XATTN_WIDTH] * (XATTN_HEAD_DIM ** -0.5)).T.astype(_BF16)
    mv = mkv[:, XATTN_WIDTH:]
    lane_head = _group_index(lax.broadcasted_iota(jnp.int32, (N_MEM, XATTN_WIDTH), 1), XATTN_HEAD_DIM)
    for h in range(N_XATTN_HEADS):
        mv_ref[h * N_MEM:(h + 1) * N_MEM, :] = jnp.where(lane_head == h, mv, 0.0).astype(_BF16)


def _cross_attention_items(z_ref, rows, mkT_ref, mv_ref, cat_ref):
    ctx = {}

    def scores():
        q = z_ref[rows, 4 * MIX_WIDTH:].astype(_BF16)
        lane_head = _group_index(lax.broadcasted_iota(jnp.int32, q.shape, 1), XATTN_HEAD_DIM)
        q_heads = [jnp.where(lane_head == h, q, jnp.zeros_like(q)) for h in range(N_XATTN_HEADS)]
        ctx["s"] = _dot(jnp.concatenate(q_heads, axis=0), mkT_ref[...])

    def softmax(h):
        def run():
            r = ctx["s"].shape[0] // N_XATTN_HEADS
            s = ctx["s"][h * r:(h + 1) * r]
            e = jnp.exp(s - jnp.max(s, axis=-1, keepdims=True))
            ctx[h] = (e / jnp.sum(e, axis=-1, keepdims=True)).astype(_BF16)
        return run

    def outputs():
        p = jnp.concatenate([ctx.pop(h) for h in range(N_XATTN_HEADS)], axis=1)
        cat_ref[rows, MIX_WIDTH:] = _dot(p, mv_ref[...])

    return [scores] + [softmax(h) for h in range(N_XATTN_HEADS)] + [outputs]


def _sub_rows(a, c):
    return a[c * HGRN_SUB:(c + 1) * HGRN_SUB]


def _scale_subs(a, vecs):
    parts = []
    for c, vec in enumerate(vecs):
        part = jnp.zeros((HGRN_SUB, a.shape[1]), _F32) if vec is None else _sub_rows(a, c) * vec
        parts.append(part.astype(_BF16))
    return jnp.concatenate(parts, axis=0)


def _hgrn_items(z_ref, rows, lb_ref, onorm_ref, st_ref, cat_ref):
    n = HGRN_BLOCK
    exp_clamp = math.exp(EXP_CLAMP)
    ops = {}
    outs = {}

    def masks():
        ti = lax.broadcasted_iota(jnp.int32, (n, n), 0)
        si = lax.broadcasted_iota(jnp.int32, (n, n), 1)
        mask_a = (_group_index(ti, HGRN_SUB) == _group_index(si, HGRN_SUB)) & (si <= ti)
        mask_b = _group_index(ti, 2 * HGRN_SUB) == _group_index(si, 2 * HGRN_SUB)
        return mask_a, mask_b, si <= ti

    def gates(h):
        def run():
            hs = _head_lanes(h)
            za = z_ref[rows, hs]
            fr = z_ref[rows, MIX_WIDTH + h * HEAD_DIM:MIX_WIDTH + (h + 1) * HEAD_DIM]
            lb = lb_ref[:, hs]
            q = za / (1.0 + jnp.exp(-za))
            e = jnp.exp(-jnp.abs(fr))
            r = 1.0 / (1.0 + e)
            nonneg = fr >= 0.0
            k = (1.0 - lb) * (jnp.where(nonneg, e, 1.0) * r)
            e_neg = jnp.where(nonneg, e, jnp.minimum(1.0 / e, exp_clamp))
            g = jnp.minimum(fr, 0.0) + jnp.log((1.0 + lb * e_neg) * r)
            g_hi = g.astype(_BF16)
            g_lo = (g - g_hi.astype(_F32)).astype(_BF16)
            lower_ones = jnp.where(masks()[2], 1.0, 0.0).astype(_BF16)
            bb = _dot(lower_ones, jnp.concatenate([g_hi, g_lo], axis=1))
            b = bb[:, :HEAD_DIM] + bb[:, HEAD_DIM:]

            mids = [b[c * HGRN_SUB + HGRN_SUB // 2 - 1:c * HGRN_SUB + HGRN_SUB // 2, :] for c in range(N_SUB)]
            ends = [b[(c + 1) * HGRN_SUB - 1:(c + 1) * HGRN_SUB, :] for c in range(N_SUB)]
            ref = jnp.concatenate([jnp.broadcast_to(m, (HGRN_SUB, HEAD_DIM)) for m in mids], axis=0)
            q_a = q * jnp.exp(b - ref)
            k_a = k * jnp.exp(ref - b)
            ops[h] = dict(
                q_a=q_a.astype(_BF16), k_a=k_a.astype(_BF16),
                q_b=_scale_subs(q_a, [None, jnp.exp(mids[1] - ends[0]), None, jnp.exp(mids[3] - ends[2])]),
                k_b=_scale_subs(k_a, [jnp.exp(ends[0] - mids[0]), None, jnp.exp(ends[2] - mids[2]), None]),
                q_c=_scale_subs(q_a, [None, None, jnp.exp(mids[2] - ends[1]), jnp.exp(mids[3] - ends[1])]),
                k_c=_scale_subs(k_a, [jnp.exp(ends[1] - mids[0]), jnp.exp(ends[1] - mids[1]), None, None]),
                q_o=_scale_subs(q_a, [jnp.exp(m) for m in mids]),
                k_s=_scale_subs(k_a, [jnp.exp(ends[N_SUB - 1] - m) for m in mids]),
                decay=jnp.exp(ends[N_SUB - 1]),
            )
        return run

    def mix(h):
        def run():
            op = ops.pop(h)
            mask_a, mask_b, _ = masks()
            vb = z_ref[rows, 2 * MIX_WIDTH + h * HEAD_DIM:2 * MIX_WIDTH + (h + 1) * HEAD_DIM].astype(_BF16)
            s_a = _dot_nt(op["q_a"], op["k_a"])
            s_b = _dot_nt(op["q_b"], op["k_b"])
            s_c = _dot_nt(op["q_c"], op["k_c"])
            scores = jnp.where(mask_a, s_a, 0.0) + jnp.where(mask_b, s_b, 0.0) + s_c
            st = st_ref[h]
            outs[h] = _dot(scores.astype(_BF16), vb) + _dot_nt(op["q_o"], st.astype(_BF16))
            st_ref[h] = st * op["decay"] + _dot_tn(vb, op["k_s"])
        return run

    def finish():
        ss = None
        for h in range(MIX_HEADS):
            part = jnp.sum(outs[h] * outs[h], axis=-1, keepdims=True)
            ss = part if ss is None else ss + part
        inv = lax.rsqrt(ss * (1.0 / MIX_WIDTH) + EPS)
        for h in range(MIX_HEADS):
            hs = _head_lanes(h)
            zg = z_ref[rows, 3 * MIX_WIDTH + h * HEAD_DIM:3 * MIX_WIDTH + (h + 1) * HEAD_DIM]
            cat_ref[rows, hs] = outs[h] * (inv * onorm_ref[:, hs]) / (1.0 + jnp.exp(-zg))

    heads = range(MIX_HEADS)
    return [gates(h) for h in heads] + [mix(h) for h in heads] + [finish]


def _retention_tables(dec_ref, qdec_ref, kdec_ref):
    n = RET_BLOCK
    scale = HEAD_DIM ** -0.5
    ti = lax.broadcasted_iota(jnp.int32, (n, n), 0)
    si = lax.broadcasted_iota(jnp.int32, (n, n), 1)
    rel = (ti - si).astype(_F32)
    pos = lax.broadcasted_iota(jnp.int32, (n, HEAD_DIM), 0).astype(_F32)
    for h in range(MIX_HEADS):
        lg = math.log(1.0 - 2.0 ** (-5.0 - h))
        dec_ref[h] = jnp.where(ti >= si, jnp.exp(lg * jnp.maximum(rel, 0.0)) * scale, 0.0)
        qdec_ref[h] = jnp.exp(lg * (pos + 1.0))
        kdec_ref[h] = jnp.exp(lg * (n - 1.0 - pos)) * scale


def _retention_items(z_ref, rows, cs_ref, onorm_ref, st_ref, cat_ref, dec_ref, qdec_ref, kdec_ref):
    n = RET_BLOCK

    def head(h):
        def run():
            cos2 = cs_ref[0, rows, :]
            sin2 = cs_ref[1, rows, :]

            def rope(t):
                return t * cos2 + pltpu.roll(t, HEAD_DIM // 2, 1) * sin2

            hs = _head_lanes(h)
            q = rope(z_ref[rows, hs])
            k = rope(z_ref[rows, MIX_WIDTH + h * HEAD_DIM:MIX_WIDTH + (h + 1) * HEAD_DIM])
            v = z_ref[rows, 2 * MIX_WIDTH + h * HEAD_DIM:2 * MIX_WIDTH + (h + 1) * HEAD_DIM]
            zg = z_ref[rows, 3 * MIX_WIDTH + h * HEAD_DIM:3 * MIX_WIDTH + (h + 1) * HEAD_DIM]
            qb = q.astype(_BF16)
            vb = v.astype(_BF16)
            scores = _dot_nt(qb, k.astype(_BF16)) * dec_ref[h]
            st = st_ref[h]
            o = _dot(scores.astype(_BF16), vb) + _dot_nt(qb, st.astype(_BF16)) * qdec_ref[h]
            k_s = k * kdec_ref[h]
            st_ref[h] = st * math.exp(_log_gamma(h) * n) + _dot_tn(vb, k_s.astype(_BF16))
            inv = lax.rsqrt(jnp.mean(o * o, axis=-1, keepdims=True) + EPS)
            cat_ref[rows, hs] = o * (inv * onorm_ref[:, hs]) * (zg / (1.0 + jnp.exp(-zg)))
        return run

    return [head(h) for h in range(MIX_HEADS)]


def _rotary_item(pos_ref, crows, freq_ref, cs_ref):
    def run():
        half_rows = (crows.stop - crows.start) // 2
        pos = pos_ref[0, crows, :].astype(_F32)
        lane = lax.broadcasted_iota(jnp.int32, (half_rows, HEAD_DIM), 1)
        low = lane < HEAD_DIM // 2
        ang = jnp.where(low, pos[:half_rows], pos[half_rows:]) * freq_ref[...]
        cos, sin = jnp.cos(ang), jnp.sin(ang)
        cos_sw, sin_sw = pltpu.roll(cos, HEAD_DIM // 2, 1), pltpu.roll(sin, HEAD_DIM // 2, 1)
        cs_ref[0, :half_rows, :] = jnp.where(low, cos, cos_sw)
        cs_ref[0, half_rows:, :] = jnp.where(low, cos_sw, cos)
        cs_ref[1, :half_rows, :] = jnp.where(low, -sin, sin_sw)
        cs_ref[1, half_rows:, :] = jnp.where(low, -sin_sw, sin)
    return run


def _in_proj_items(x_ref, crows, nmix_ref, win_ref, hn_ref, z_ref):
    def norm():
        hn_ref[...] = _rmsnorm(x_ref[0, crows, :], nmix_ref[...])

    def cols(p):
        def run():
            cs = slice(p * PROJ_COLS, (p + 1) * PROJ_COLS)
            z_ref[:, cs] = _dot(hn_ref[...], win_ref[:, cs])
        return run

    return [norm] + [cols(p) for p in range(IN_WIDTH // PROJ_COLS)]


def _out_proj_items(x_ref, crows, cat_ref, wout_ref, out_ref):
    def cols(p):
        def run():
            cs = slice(p * PROJ_COLS, (p + 1) * PROJ_COLS)
            out_ref[0, crows, cs] = x_ref[0, crows, cs] + _dot(cat_ref[...], wout_ref[:, cs])
        return run

    return [cols(p) for p in range(D_MODEL // PROJ_COLS)]


def _run_interleaved(main, filler):
    done = 0
    for k, item in enumerate(main):
        item()
        upto = (k + 1) * len(filler) // len(main)
        for f in filler[done:upto]:
            f()
        done = upto
    for f in filler[done:]:
        f()


def _mixer_kernel(*refs, kind):
    if kind == "hgrn":
        (x_ref, mem_ref, nmix_ref, win_ref, wout_ref, nmem_ref, wkv_ref, lb_ref, onorm_ref,
         out_ref, z_ref, hn_ref, cat_ref, st_ref, mkT_ref, mv_ref) = refs
    else:
        (x_ref, mem_ref, nmix_ref, win_ref, wout_ref, nmem_ref, wkv_ref, pos_ref, freq_ref, onorm_ref,
         out_ref, z_ref, hn_ref, cat_ref, st_ref, mkT_ref, mv_ref, cs_ref, dec_ref, qdec_ref, kdec_ref) = refs

    @pl.when(pl.program_id(1) == 0)
    def _start_of_sequence():
        st_ref[...] = jnp.zeros_like(st_ref)
        _memory_kv(mem_ref, nmem_ref, wkv_ref, mkT_ref, mv_ref)
        if kind != "hgrn":
            _retention_tables(dec_ref, qdec_ref, kdec_ref)

    chunk = CHUNK[kind]
    n_chunks = SEQ_TILE[kind] // chunk
    n_slots = z_ref.shape[0]
    cat_slots = cat_ref.shape[0]

    def chunk_rows(c):
        return slice(c * chunk, (c + 1) * chunk)

    def stage_in(c):
        return _in_proj_items(x_ref, chunk_rows(c), nmix_ref, win_ref, hn_ref, z_ref.at[c % n_slots])

    def stage_mix(c):
        zc, cc = z_ref.at[c % n_slots], cat_ref.at[c % cat_slots]
        whole = slice(0, chunk)
        xattn = _cross_attention_items(zc, whole, mkT_ref, mv_ref, cc)
        if kind == "hgrn":
            items = []
            for i in range(chunk // HGRN_BLOCK):
                rows = slice(i * HGRN_BLOCK, (i + 1) * HGRN_BLOCK)
                items += _hgrn_items(zc, rows, lb_ref, onorm_ref, st_ref, cc)
            return items + xattn
        items = []
        for i in range(chunk // RET_BLOCK):
            rows = slice(i * RET_BLOCK, (i + 1) * RET_BLOCK)
            items += _retention_items(zc, rows, cs_ref.at[c], onorm_ref, st_ref, cc, dec_ref, qdec_ref, kdec_ref)
        return items + xattn

    def stage_out(c):
        return _out_proj_items(x_ref, chunk_rows(c), cat_ref.at[c % cat_slots], wout_ref, out_ref)

    if kind != "hgrn":
        for c in range(n_chunks):
            _rotary_item(pos_ref, chunk_rows(c), freq_ref, cs_ref.at[c])()
    if n_slots == 1:
        for c in range(n_chunks):
            for item in stage_in(c):
                item()
            _run_interleaved(stage_mix(c), stage_out(c - 1) if c > 0 else [])
        for item in stage_out(n_chunks - 1):
            item()
        return
    for item in stage_in(0):
        item()
    for c in range(n_chunks):
        filler = (stage_out(c - 1) if c > 0 else []) + (stage_in(c + 1) if c + 1 < n_chunks else [])
        _run_interleaved(stage_mix(c), filler)
    for item in stage_out(n_chunks - 1):
        item()


def _mixer_layer(kind, layer, x, mem, norm_mix, w_in, w_out, norm_mem, w_mem_kv, extra):
    bsz, seq, _ = x.shape
    seq_tile = SEQ_TILE[kind]
    const2 = lambda b, t: (0, 0)
    layer3 = lambda b, t: (layer, 0, 0)
    in_specs = [
        pl.BlockSpec((1, seq_tile, D_MODEL), lambda b, t: (b, t, 0)),
        pl.BlockSpec((1, N_MEM, D_MODEL), lambda b, t: (b, 0, 0)),
        pl.BlockSpec((1, D_MODEL), const2),
        pl.BlockSpec((None, D_MODEL, IN_WIDTH), layer3, pipeline_mode=pl.Buffered(1)),
        pl.BlockSpec((None, CAT_WIDTH, D_MODEL), layer3, pipeline_mode=pl.Buffered(1)),
        pl.BlockSpec((1, D_MODEL), const2),
        pl.BlockSpec((None, D_MODEL, 2 * XATTN_WIDTH), layer3, pipeline_mode=pl.Buffered(1)),
    ]
    chunk = CHUNK[kind]
    n_slots = min(CHUNK_SLOTS[kind], seq_tile // chunk)
    scratch = [
        pltpu.VMEM((n_slots, chunk, IN_WIDTH), _F32),
        pltpu.VMEM((chunk, D_MODEL), _F32),
        pltpu.VMEM((min(2, seq_tile // chunk), chunk, CAT_WIDTH), _F32),
        pltpu.VMEM((MIX_HEADS, HEAD_DIM, HEAD_DIM), _F32),
        pltpu.VMEM((XATTN_WIDTH, N_MEM), _BF16),
        pltpu.VMEM((N_XATTN_HEADS * N_MEM, XATTN_WIDTH), _BF16),
    ]
    if kind == "hgrn":
        in_specs += [pl.BlockSpec((1, MIX_WIDTH), const2), pl.BlockSpec((1, MIX_WIDTH), const2)]
    else:
        in_specs += [
            pl.BlockSpec((1, seq_tile, 1), lambda b, t: (b, t, 0)),
            pl.BlockSpec((1, HEAD_DIM), const2),
            pl.BlockSpec((1, MIX_WIDTH), const2),
        ]
        scratch += [
            pltpu.VMEM((seq_tile // chunk, 2, chunk, HEAD_DIM), _F32),
            pltpu.VMEM((MIX_HEADS, RET_BLOCK, RET_BLOCK), _F32),
            pltpu.VMEM((MIX_HEADS, RET_BLOCK, HEAD_DIM), _F32),
            pltpu.VMEM((MIX_HEADS, RET_BLOCK, HEAD_DIM), _F32),
        ]
    return pl.pallas_call(
        functools.partial(_mixer_kernel, kind=kind),
        grid=(bsz, seq // seq_tile),
        in_specs=in_specs,
        out_specs=pl.BlockSpec((1, seq_tile, D_MODEL), lambda b, t: (b, t, 0)),
        out_shape=jax.ShapeDtypeStruct(x.shape, x.dtype),
        scratch_shapes=scratch,
        compiler_params=pltpu.CompilerParams(
            dimension_semantics=("arbitrary", "arbitrary"), vmem_limit_bytes=VMEM_LIMIT_BYTES),
        name=f"{kind}_mixer_layer",
    )(x, mem, norm_mix, w_in, w_out, norm_mem, w_mem_kv, *extra)


def _ffn_kernel(*refs, final):
    if final:
        x_ref, n_ref, win_ref, wout_ref, nf_ref, out_ref = refs
    else:
        x_ref, n_ref, win_ref, wout_ref, out_ref = refs
    x = x_ref[...]
    hn = _rmsnorm(x, n_ref[...])
    acc = x
    assert sum(FFN_CHUNKS) == D_FF
    lo = 0
    for width in FFN_CHUNKS:
        hi = lo + width
        g = _dot(hn, win_ref[:, lo:hi])
        u = _dot(hn, win_ref[:, D_FF + lo:D_FF + hi])
        a = g * jax.nn.sigmoid(g) * u
        acc = acc + _dot(a, wout_ref[lo:hi, :])
        lo = hi
    if final:
        acc = _rmsnorm(acc, nf_ref[...])
    out_ref[...] = acc


def _ffn_layer(x2d, layer, norm_ffn, w_ffn_in, w_ffn_out, norm_final):
    final = norm_final is not None
    const = lambda i: (0, 0)
    in_specs = [
        pl.BlockSpec((FFN_TILE, D_MODEL), lambda i: (i, 0)),
        pl.BlockSpec((1, D_MODEL), const),
        pl.BlockSpec((None, D_MODEL, 2 * D_FF), lambda i: (layer, 0, 0), pipeline_mode=pl.Buffered(1)),
        pl.BlockSpec((None, D_FF, D_MODEL), lambda i: (layer, 0, 0), pipeline_mode=pl.Buffered(1)),
    ]
    args = [x2d, norm_ffn, w_ffn_in, w_ffn_out]
    if final:
        in_specs.append(pl.BlockSpec((1, D_MODEL), const))
        args.append(norm_final)
    return pl.pallas_call(
        functools.partial(_ffn_kernel, final=final),
        grid=(x2d.shape[0] // FFN_TILE,),
        in_specs=in_specs,
        out_specs=pl.BlockSpec((FFN_TILE, D_MODEL), lambda i: (i, 0)),
        out_shape=jax.ShapeDtypeStruct(x2d.shape, x2d.dtype),
        compiler_params=pltpu.CompilerParams(
            dimension_semantics=("arbitrary",), vmem_limit_bytes=VMEM_LIMIT_BYTES),
        name="swiglu_ffn_final" if final else "swiglu_ffn",
    )(*args)


def kernel(x, mem, positions, norm_mix, w_in, w_out, norm_mem, w_mem_kv, hgrn_lb_logits, hgrn_out_norm,
           ret_out_norm, norm_ffn, w_ffn_in, w_ffn_out, norm_final):
    bsz, seq, _ = x.shape
    depth = w_in.shape[0]
    assert all(seq % tile == 0 for tile in SEQ_TILE.values()) and (bsz * seq) % FFN_TILE == 0

    p_lb = jax.nn.softmax(hgrn_lb_logits.astype(_F32), axis=0)
    lower_bounds = jnp.cumsum(p_lb, axis=0) - p_lb[0]
    half = HEAD_DIM // 2
    inv_freq = ROPE_BASE ** (-jnp.linspace(0.0, 1.0, half, dtype=_F32))
    inv_freq2 = jnp.concatenate([inv_freq, inv_freq]).reshape(1, HEAD_DIM)
    pos3 = positions.reshape(bsz, seq, 1)

    for i in range(depth):
        j = i // N_MIXERS
        common = (i, x, mem, norm_mix[i].reshape(1, D_MODEL), w_in, w_out, norm_mem[i].reshape(1, D_MODEL), w_mem_kv)
        if i % N_MIXERS == 0:
            extra = (lower_bounds[j].reshape(1, MIX_WIDTH), hgrn_out_norm[j].reshape(1, MIX_WIDTH))
            x = _mixer_layer("hgrn", *common, extra)
        else:
            extra = (pos3, inv_freq2, ret_out_norm[j].reshape(1, MIX_WIDTH))
            x = _mixer_layer("retention", *common, extra)
        x = _ffn_layer(
            x.reshape(bsz * seq, D_MODEL), i, norm_ffn[i].reshape(1, D_MODEL), w_ffn_in, w_ffn_out,
            norm_final.reshape(1, D_MODEL) if i == depth - 1 else None,
        ).reshape(bsz, seq, D_MODEL)
    return x
```
